```python
import jax, jax.numpy as jnp
from jax import lax
import numpy as np

D_MODEL = 1024
BATCH = 16
SEQ = 4096
DEPTH = 1

HEAD_DIM = 64
N_HEADS_SB = 8
N_HEADS_FOX = 8
D_SB = N_HEADS_SB * HEAD_DIM
D_FOX = N_HEADS_FOX * HEAD_DIM
D_FF = -(-8 * D_MODEL // (3 * 256)) * 256
Q_BLOCK = 128
RMS_EPS = 1e-6
N_MOD = 6
FORGET_BIAS_INIT = 3.0
IN_SPLITS = (D_SB, 2 * D_SB, 3 * D_SB,
             3 * D_SB + D_FOX, 3 * D_SB + 2 * D_FOX, 3 * D_SB + 3 * D_FOX,
             3 * D_SB + 3 * D_FOX + N_HEADS_FOX,
             3 * D_SB + 3 * D_FOX + N_HEADS_FOX + D_MODEL)
D_IN_PROJ = 3 * D_SB + 3 * D_FOX + N_HEADS_FOX + 2 * D_MODEL

kernel_name = 'hybrid_stickbreak_fox_gated_block'


def rms_norm(x, g):
    xf = x.astype(jnp.float32)
    y = xf * lax.rsqrt(jnp.mean(xf * xf, axis=-1, keepdims=True) + RMS_EPS)
    return (y * g.astype(jnp.float32)).astype(x.dtype)


def to_blocks(t):
    b, s, h, d = t.shape
    return t.reshape(b, s // Q_BLOCK, Q_BLOCK, h, d).transpose(1, 0, 3, 2, 4)


def from_blocks(t):
    nb, b, h, q, d = t.shape
    return t.transpose(1, 0, 3, 2, 4).reshape(b, nb * q, h * d)


def stick_breaking_attention(q, k, v):
    s = q.shape[1]
    nb = s // Q_BLOCK
    kh = k.transpose(0, 2, 1, 3).astype(jnp.float32)
    vh = v.transpose(0, 2, 1, 3).astype(jnp.float32)
    key_pos = jnp.arange(s, dtype=jnp.int32)
    scale = HEAD_DIM ** -0.5

    def block(args):
        q_blk, start = args
        z = jnp.einsum('bhqd,bhkd->bhqk', q_blk.astype(jnp.float32), kh) * scale
        q_pos = start + jnp.arange(Q_BLOCK, dtype=jnp.int32)
        past = key_pos[None, :] < q_pos[:, None]
        log_keep = jnp.where(past, jax.nn.log_sigmoid(-z), 0.0)
        after = lax.cumsum(log_keep, axis=3, reverse=True) - log_keep
        w = jnp.where(past, jnp.exp(jax.nn.log_sigmoid(z) + after), 0.0)
        return jnp.einsum('bhqk,bhkd->bhqd', w, vh)

    starts = jnp.arange(nb, dtype=jnp.int32) * Q_BLOCK
    out = lax.map(block, (to_blocks(q), starts))
    return from_blocks(out).astype(q.dtype)


def forgetting_attention(q, k, v, log_f):
    b, s, h, _ = q.shape
    nb = s // Q_BLOCK
    cum = jnp.cumsum(log_f.astype(jnp.float32), axis=1)
    cum_k = cum.transpose(0, 2, 1)
    cum_q = cum.reshape(b, nb, Q_BLOCK, h).transpose(1, 0, 3, 2)
    kh = k.transpose(0, 2, 1, 3).astype(jnp.float32)
    vh = v.transpose(0, 2, 1, 3).astype(jnp.float32)
    key_pos = jnp.arange(s, dtype=jnp.int32)
    scale = HEAD_DIM ** -0.5

    def block(args):
        q_blk, cq, start = args
        logits = jnp.einsum('bhqd,bhkd->bhqk', q_blk.astype(jnp.float32), kh) * scale
        logits = logits + cq[..., :, None] - cum_k[:, :, None, :]
        q_pos = start + jnp.arange(Q_BLOCK, dtype=jnp.int32)
        causal = key_pos[None, :] <= q_pos[:, None]
        p = jax.nn.softmax(jnp.where(causal, logits, -jnp.inf), axis=-1)
        return jnp.einsum('bhqk,bhkd->bhqd', p, vh)

    starts = jnp.arange(nb, dtype=jnp.int32) * Q_BLOCK
    out = lax.map(block, (to_blocks(q), cum_q, starts))
    return from_blocks(out).astype(q.dtype)


def setup_inputs(seed: int = 0) -> dict:
    key = jax.random.key(seed)
    ks = jax.random.split(key, 17)
    f32 = jnp.float32

    def nrm(k, shape, fan_in):
        return jax.random.normal(k, shape, f32) * fan_in ** -0.5

    def gain(k, shape):
        return 1.0 + 0.05 * jax.random.normal(k, shape, f32)

    return {
        'x': jax.random.normal(ks[0], (BATCH, SEQ, D_MODEL), f32),
        'c': jax.random.normal(ks[1], (BATCH, D_MODEL), f32),
        'w_ada': nrm(ks[2], (DEPTH, D_MODEL, N_MOD * D_MODEL), D_MODEL),
        'b_ada': 0.02 * jax.random.normal(ks[3], (DEPTH, N_MOD * D_MODEL), f32),
        'g_mix': gain(ks[4], (DEPTH, D_MODEL)),
        'w_in': nrm(ks[5], (DEPTH, D_MODEL, D_IN_PROJ), D_MODEL),
        'b_forget': FORGET_BIAS_INIT + 0.1 * jax.random.normal(ks[6], (DEPTH, N_HEADS_FOX), f32),
        'b_gate': 0.02 * jax.random.normal(ks[7], (DEPTH, 2 * D_MODEL), f32),
        'w_branch_sb': nrm(ks[8], (DEPTH, D_SB, D_MODEL), D_SB),
        'w_branch_fox': nrm(ks[9], (DEPTH, D_FOX, D_MODEL), D_FOX),
        'w_out': nrm(ks[10], (DEPTH, D_MODEL, D_MODEL), D_MODEL),
        'g_ffn': gain(ks[11], (DEPTH, D_MODEL)),
        'w_ffn_gate': nrm(ks[12], (DEPTH, D_MODEL, D_FF), D_MODEL),
        'w_ffn_up': nrm(ks[13], (DEPTH, D_MODEL, D_FF), D_MODEL),
        'w_ffn_down': nrm(ks[14], (DEPTH, D_FF, D_MODEL), D_FF),
        'g_final': gain(ks[15], (D_MODEL,)),
    }


def reference(x, c, w_ada, b_ada, g_mix, w_in, b_forget, b_gate, w_branch_sb, w_branch_fox,
              w_out, g_ffn, w_ffn_gate, w_ffn_up, w_ffn_down, g_final):
    b, s, _ = x.shape
    c_act = jax.nn.silu(c)
    for l in range(DEPTH):
        mod = c_act @ w_ada[l] + b_ada[l]
        shift1, scale1, gate1, shift2, scale2, gate2 = [m[:, None, :] for m in jnp.split(mod, N_MOD, axis=-1)]

        h = rms_norm(x, g_mix[l]) * (1.0 + scale1) + shift1
        proj = h @ w_in[l]
        q_sb, k_sb, v_sb, q_fx, k_fx, v_fx, f_logit, gl_sb, gl_fx = jnp.split(proj, IN_SPLITS, axis=-1)
        heads_sb = lambda t: t.reshape(b, s, N_HEADS_SB, HEAD_DIM)
        heads_fx = lambda t: t.reshape(b, s, N_HEADS_FOX, HEAD_DIM)

        y_sb = stick_breaking_attention(heads_sb(q_sb), heads_sb(k_sb), heads_sb(v_sb))
        log_f = jax.nn.log_sigmoid(f_logit.astype(jnp.float32) + b_forget[l])
        y_fx = forgetting_attention(heads_fx(q_fx), heads_fx(k_fx), heads_fx(v_fx), log_f)

        gates = jax.nn.sigmoid(jnp.concatenate([gl_sb, gl_fx], axis=-1) + b_gate[l])
        g_sb, g_fx = jnp.split(gates, 2, axis=-1)
        merged = g_sb * (y_sb @ w_branch_sb[l]) + g_fx * (y_fx @ w_branch_fox[l])
        x = x + gate1 * (merged @ w_out[l])

        h2 = rms_norm(x, g_ffn[l]) * (1.0 + scale2) + shift2
        ffn = (jax.nn.silu(h2 @ w_ffn_gate[l]) * (h2 @ w_ffn_up[l])) @ w_ffn_down[l]
        x = x + gate2 * ffn
    return rms_norm(x, g_final)
```

```python
import functools
import math

import jax
import jax.numpy as jnp
from jax import lax
from jax.experimental import pallas as pl
from jax.experimental.pallas import tpu as pltpu

F32 = jnp.float32
BF16 = jnp.bfloat16

HEAD_DIM = 64
N_HEADS = 8
D_HEADS = N_HEADS * HEAD_DIM
N_MOD = 6
RMS_EPS = 1e-6
LOG2E = math.log2(math.e)
Q_SCALE = LOG2E * HEAD_DIM ** -0.5
FOX_K = 128
NEG_BIG = -1e30
SB_DEAD_LOG2 = 160.0
V7X_VMEM_LIMIT = 56 * 1024 * 1024


def _rms_modulate(x, g, scale, shift):
    ms = jnp.mean(x * x, axis=-1, keepdims=True)
    y = x * lax.rsqrt(ms + RMS_EPS) * g
    return y * (1.0 + scale) + shift


def _const_spec(shape):
    return pl.BlockSpec(shape, lambda *_: (0,) * len(shape), pipeline_mode=pl.Buffered(1))


def _mod_kernel(c_ref, w_ref, b_ref, o_ref):
    c = c_ref[...]
    ca = c * jax.nn.sigmoid(c)
    o_ref[...] = jnp.dot(ca, w_ref[...], preferred_element_type=F32,
                         precision=lax.Precision.HIGHEST) + b_ref[...]


def _mod_call(c, w_ada, b_ada):
    b, d = c.shape
    n = w_ada.shape[1]
    tn = 1024
    return pl.pallas_call(
        _mod_kernel,
        grid=(n // tn,),
        in_specs=[pl.BlockSpec((b, d), lambda j: (0, 0)),
                  pl.BlockSpec((d, tn), lambda j: (0, j)),
                  pl.BlockSpec((1, tn), lambda j: (0, j))],
        out_specs=pl.BlockSpec((b, tn), lambda j: (0, j)),
        out_shape=jax.ShapeDtypeStruct((b, n), F32),
        name="mod",
    )(c, w_ada, b_ada.reshape(1, n))


def _inproj_kernel(x_ref, mod_ref, g_ref, wa_ref, wf_ref, wg_ref, bf_ref, bg_ref,
                   qs_ref, ks_ref, vs_ref, qf_ref, kf_ref, vf_ref, lf_ref, gate_ref):
    mod = mod_ref[...]
    h = _rms_modulate(x_ref[...], g_ref[...], mod[1:2, :], mod[0:1, :]).astype(BF16)
    a = jnp.dot(h, wa_ref[...], preferred_element_type=F32)
    d = D_HEADS
    qs_ref[...] = (a[:, 0 * d:1 * d] * Q_SCALE).astype(BF16)
    ks_ref[...] = a[:, 1 * d:2 * d].astype(BF16)
    vs_ref[...] = a[:, 2 * d:3 * d].astype(BF16)
    qf_ref[...] = (a[:, 3 * d:4 * d] * Q_SCALE).astype(BF16)
    kf_ref[...] = a[:, 4 * d:5 * d].astype(BF16)
    vf_ref[...] = a[:, 5 * d:6 * d].astype(BF16)
    f = jnp.dot(h, wf_ref[...], preferred_element_type=F32) + bf_ref[...]
    lf = -(jnp.maximum(-f, 0.0) + jnp.log(1.0 + jnp.exp(-jnp.abs(f)))) * LOG2E
    lf_ref[...] = lf[:, :N_HEADS]
    gl = jnp.dot(h, wg_ref[...], preferred_element_type=F32) + bg_ref[...]
    gate_ref[...] = jax.nn.sigmoid(gl).astype(BF16)


def _inproj_call(x, mod3, g_mix, w_a, w_f, w_g, b_f, b_g, tm):
    b, s, d = x.shape
    row = lambda bi, i: (bi, i, 0)
    head_out = pl.BlockSpec((None, tm, D_HEADS), row)
    return pl.pallas_call(
        _inproj_kernel,
        grid=(b, s // tm),
        in_specs=[pl.BlockSpec((None, tm, d), row),
                  pl.BlockSpec((None, N_MOD, d), lambda bi, i: (bi, 0, 0)),
                  _const_spec((1, d)),
                  _const_spec(w_a.shape), _const_spec(w_f.shape), _const_spec(w_g.shape),
                  _const_spec(b_f.shape), _const_spec(b_g.shape)],
        out_specs=[head_out] * 6 + [pl.BlockSpec((None, tm, N_HEADS), row),
                                    pl.BlockSpec((None, tm, 2 * d), row)],
        out_shape=[jax.ShapeDtypeStruct((b, s, D_HEADS), BF16)] * 6
        + [jax.ShapeDtypeStruct((b, s, N_HEADS), F32),
           jax.ShapeDtypeStruct((b, s, 2 * d), BF16)],
        compiler_params=pltpu.CompilerParams(
            dimension_semantics=("parallel", "parallel"), vmem_limit_bytes=V7X_VMEM_LIMIT),
        name="inproj",
    )(x, mod3, g_mix, w_a, w_f, w_g, b_f, b_g)


def _split3(v):
    hi = v.astype(BF16).astype(F32)
    r = v - hi
    mid = r.astype(BF16).astype(F32)
    lo = (r - mid).astype(BF16).astype(F32)
    return hi, mid, lo


def _cumsum_kernel(lf_ref, cum_ref, hi_ref, mid_ref, lo_ref, *, chunk):
    h, s = lf_ref.shape
    j = lax.broadcasted_iota(jnp.int32, (chunk, chunk), 0)
    t = lax.broadcasted_iota(jnp.int32, (chunk, chunk), 1)
    tri = (j <= t).astype(BF16)
    carry = jnp.zeros((h, 1), F32)
    for c in range(s // chunk):
        sl = slice(c * chunk, (c + 1) * chunk)
        pieces = jnp.concatenate(_split3(lf_ref[:, sl]), axis=0).astype(BF16)
        p = jnp.dot(pieces, tri, preferred_element_type=F32)
        cum = (p[0:h] + p[h:2 * h]) + p[2 * h:3 * h] + carry
        cum_ref[:, sl] = cum
        hi, mid, lo = _split3(cum)
        hi_ref[:, sl] = hi
        mid_ref[:, sl] = mid
        lo_ref[:, sl] = lo
        carry = cum[:, chunk - 1:chunk]


def _cumsum_call(lf_t):
    b, h, s = lf_t.shape
    spec = pl.BlockSpec((None, h, s), lambda bi: (bi, 0, 0))
    return pl.pallas_call(
        functools.partial(_cumsum_kernel, chunk=min(256, s)),
        grid=(b,),
        in_specs=[spec],
        out_specs=[spec] * 4,
        out_shape=[jax.ShapeDtypeStruct((b, h, s), F32)] * 4,
        compiler_params=pltpu.CompilerParams(dimension_semantics=("parallel",)),
        name="cumsum",
    )(lf_t)


def _softplus2(z):
    return jnp.maximum(z, 0.0) + jnp.log2(1.0 + jnp.exp2(-jnp.abs(z)))


def _sb_kernel(qt_ref, k_ref, vt_ref, o_ref, *, tq):
    i = pl.program_id(2)
    qt = qt_ref[...]
    rows = lax.broadcasted_iota(jnp.int32, (tq, tq), 0)
    cols = lax.broadcasted_iota(jnp.int32, (tq, tq), 1)
    suffix = (cols >= rows).astype(BF16)
    past = rows < cols

    def block(j, acc, run, masked):
        kb = k_ref[pl.ds(pl.multiple_of(j * tq, tq), tq), :]
        z = jnp.dot(kb, qt, preferred_element_type=F32)
        sp = _softplus2(z)
        if masked:
            sp = jnp.where(past, sp, 0.0)
        c = jnp.dot(suffix, sp.astype(BF16), preferred_element_type=F32)
        w = jnp.exp2(z - c - run)
        if masked:
            w = jnp.where(past, w, 0.0)
        acc = acc + jnp.dot(vt_ref[j], w.astype(BF16), preferred_element_type=F32)
        return acc, run + c[0:1, :]

    acc, run = block(i, jnp.zeros((HEAD_DIM, tq), F32), jnp.zeros((1, tq), F32), True)

    def cond(state):
        n, _, run = state
        return jnp.logical_and(n < i, jnp.min(run) < SB_DEAD_LOG2)

    def body(state):
        n, acc, run = state
        acc, run = block(i - 1 - n, acc, run, False)
        return n + 1, acc, run

    _, acc, _ = lax.while_loop(cond, body, (jnp.int32(0), acc, run))
    o_ref[...] = acc.astype(o_ref.dtype)


def _sb_call(qt, k, vt, tq):
    b, h, _, s = qt.shape
    nk = s // tq
    return pl.pallas_call(
        functools.partial(_sb_kernel, tq=tq),
        grid=(b, h, s // tq),
        in_specs=[pl.BlockSpec((None, None, HEAD_DIM, tq), lambda bi, hi, i: (bi, hi, 0, i)),
                  pl.BlockSpec((None, None, s, HEAD_DIM), lambda bi, hi, i: (bi, hi, 0, 0)),
                  pl.BlockSpec((None, None, nk, HEAD_DIM, tq), lambda bi, hi, i: (bi, hi, 0, 0, 0))],
        out_specs=pl.BlockSpec((None, None, HEAD_DIM, tq), lambda bi, hi, i: (bi, hi, 0, i)),
        out_shape=jax.ShapeDtypeStruct((b, h, HEAD_DIM, s), BF16),
        compiler_params=pltpu.CompilerParams(
            dimension_semantics=("parallel", "parallel", "arbitrary")),
        name="sb_attn",
    )(qt, k, vt)


def _fox_kernel(qt_ref, k_ref, vt_ref, o_ref, *, tq):
    i = pl.program_id(2)
    qt = qt_ref[...]
    rows = lax.broadcasted_iota(jnp.int32, (tq, tq), 0)
    cols = lax.broadcasted_iota(jnp.int32, (tq, tq), 1)
    causal = rows <= cols

    def block(j, state, masked):
        m, l, acc = state
        kb = k_ref[pl.ds(pl.multiple_of(j * tq, tq), tq), :]
        s = jnp.dot(kb, qt, preferred_element_type=F32)
        if masked:
            s = jnp.where(causal, s, NEG_BIG)
        m_new = jnp.maximum(m, jnp.max(s, axis=0, keepdims=True))
        alpha = jnp.exp2(m - m_new)
        p = jnp.exp2(s - m_new)
        l = alpha * l + jnp.sum(p, axis=0, keepdims=True)
        acc = alpha * acc + jnp.dot(vt_ref[j], p.astype(BF16), preferred_element_type=F32)
        return m_new, l, acc

    state = (jnp.full((1, tq), NEG_BIG, F32), jnp.zeros((1, tq), F32),
             jnp.zeros((HEAD_DIM, tq), F32))
    state = block(i, state, True)
    _, l, acc = lax.fori_loop(0, i, lambda j, st: block(j, st, False), state)
    o_ref[...] = (acc / l).astype(o_ref.dtype)


def _fox_call(qt, k, vt, tq):
    b, h, kd, s = qt.shape
    nk = s // tq
    return pl.pallas_call(
        functools.partial(_fox_kernel, tq=tq),
        grid=(b, h, s // tq),
        in_specs=[pl.BlockSpec((None, None, kd, tq), lambda bi, hi, i: (bi, hi, 0, i)),
                  pl.BlockSpec((None, None, s, kd), lambda bi, hi, i: (bi, hi, 0, 0)),
                  pl.BlockSpec((None, None, nk, HEAD_DIM, tq), lambda bi, hi, i: (bi, hi, 0, 0, 0))],
        out_specs=pl.BlockSpec((None, None, HEAD_DIM, tq), lambda bi, hi, i: (bi, hi, 0, i)),
        out_shape=jax.ShapeDtypeStruct((b, h, HEAD_DIM, s), BF16),
        compiler_params=pltpu.CompilerParams(
            dimension_semantics=("parallel", "parallel", "arbitrary")),
        name="fox_attn",
    )(qt, k, vt)


def _merge_kernel(x_ref, ys_ref, yf_ref, gate_ref, mod_ref, g_ref, ws_ref, wfx_ref, wo_ref,
                  x1_ref, h2_ref):
    d = x_ref.shape[-1]
    mod = mod_ref[...]
    a = jnp.dot(ys_ref[...], ws_ref[...], preferred_element_type=F32)
    b = jnp.dot(yf_ref[...], wfx_ref[...], preferred_element_type=F32)
    gate = gate_ref[...].astype(F32)
    merged = gate[:, :d] * a + gate[:, d:] * b
    x1 = x_ref[...] + mod[2:3, :] * jnp.dot(merged.astype(BF16), wo_ref[...],
                                            preferred_element_type=F32)
    x1_ref[...] = x1
    h2_ref[...] = _rms_modulate(x1, g_ref[...], mod[4:5, :], mod[3:4, :]).astype(BF16)


def _merge_call(x, y_sb, y_fx, gates, mod3, g_ffn, w_bs, w_bf, w_out, tm):
    b, s, d = x.shape
    row = lambda bi, i: (bi, i, 0)
    return pl.pallas_call(
        _merge_kernel,
        grid=(b, s // tm),
        in_specs=[pl.BlockSpec((None, tm, d), row),
                  pl.BlockSpec((None, tm, D_HEADS), row),
                  pl.BlockSpec((None, tm, D_HEADS), row),
                  pl.BlockSpec((None, tm, 2 * d), row),
                  pl.BlockSpec((None, N_MOD, d), lambda bi, i: (bi, 0, 0)),
                  _const_spec((1, d)),
                  _const_spec(w_bs.shape), _const_spec(w_bf.shape), _const_spec(w_out.shape)],
        out_specs=[pl.BlockSpec((None, tm, d), row), pl.BlockSpec((None, tm, d), row)],
        out_shape=[jax.ShapeDtypeStruct((b, s, d), F32), jax.ShapeDtypeStruct((b, s, d), BF16)],
        compiler_params=pltpu.CompilerParams(
            dimension_semantics=("parallel", "parallel"), vmem_limit_bytes=V7X_VMEM_LIMIT),
        name="merge",
    )(x, y_sb, y_fx, gates, mod3, g_ffn, w_bs, w_bf, w_out)


def _ffn_kernel(x1_ref, h2_ref, mod_ref, g_ref, wg_ref, wu_ref, wd_ref, o_ref, *, n_chunks):
    h2 = h2_ref[...]
    dff = wg_ref.shape[1]
    cw = dff // n_chunks
    ffn = None
    for c in range(n_chunks):
        sl = slice(c * cw, (c + 1) * cw)
        gt = jnp.dot(h2, wg_ref[:, sl], preferred_element_type=F32)
        up = jnp.dot(h2, wu_ref[:, sl], preferred_element_type=F32)
        act = (gt * jax.nn.sigmoid(gt) * up).astype(BF16)
        part = jnp.dot(act, wd_ref[sl, :], preferred_element_type=F32)
        ffn = part if ffn is None else ffn + part
    x2 = x1_ref[...] + mod_ref[5:6, :] * ffn
    ms = jnp.mean(x2 * x2, axis=-1, keepdims=True)
    o_ref[...] = x2 * lax.rsqrt(ms + RMS_EPS) * g_ref[...]


def _ffn_call(x1, h2, mod3, g_final, w_gate, w_up, w_down, tm):
    b, s, d = x1.shape
    row = lambda bi, i: (bi, i, 0)
    dff = w_gate.shape[1]
    n_chunks = 2 if dff % 256 == 0 else 1
    return pl.pallas_call(
        functools.partial(_ffn_kernel, n_chunks=n_chunks),
        grid=(b, s // tm),
        in_specs=[pl.BlockSpec((None, tm, d), row),
                  pl.BlockSpec((None, tm, d), row),
                  pl.BlockSpec((None, N_MOD, d), lambda bi, i: (bi, 0, 0)),
                  _const_spec((1, d)),
                  _const_spec(w_gate.shape), _const_spec(w_up.shape), _const_spec(w_down.shape)],
        out_specs=pl.BlockSpec((None, tm, d), row),
        out_shape=jax.ShapeDtypeStruct((b, s, d), F32),
        compiler_params=pltpu.CompilerParams(
            dimension_semantics=("parallel", "parallel"), vmem_limit_bytes=V7X_VMEM_LIMIT),
        name="ffn",
    )(x1, h2, mod3, g_final, w_gate, w_up, w_down)


def _heads_t(t):
    b, s, _ = t.shape
    return t.reshape(b, s, N_HEADS, HEAD_DIM).transpose(0, 2, 3, 1)


def _heads(t):
    b, s, _ = t.shape
    return t.reshape(b, s, N_HEADS, HEAD_DIM).transpose(0, 2, 1, 3)


def _value_blocks(t, tk):
    b, s, _ = t.shape
    return t.reshape(b, s // tk, tk, N_HEADS, HEAD_DIM).transpose(0, 3, 1, 4, 2)


def _unheads_t(t):
    b, h, dh, s = t.shape
    return t.transpose(0, 3, 1, 2).reshape(b, s, h * dh)


def kernel(x, c, w_ada, b_ada, g_mix, w_in, b_forget, b_gate, w_branch_sb, w_branch_fox,
           w_out, g_ffn, w_ffn_gate, w_ffn_up, w_ffn_down, g_final):
    b, s, d = x.shape
    depth = w_ada.shape[0]
    assert depth == 1, "the final RMSNorm is fused into the last (only) layer's FFN call"
    tm = min(512, s)
    tq = min(256, s)
    n_qkv = 6 * D_HEADS
    for l in range(depth):
        mod3 = _mod_call(c, w_ada[l], b_ada[l]).reshape(b, N_MOD, d)
        w_a = w_in[l, :, :n_qkv].astype(BF16)
        w_f = jnp.pad(w_in[l, :, n_qkv:n_qkv + N_HEADS], ((0, 0), (0, 128 - N_HEADS))).astype(BF16)
        w_g = w_in[l, :, n_qkv + N_HEADS:].astype(BF16)
        b_f = jnp.pad(b_forget[l], (0, 128 - N_HEADS)).reshape(1, 128)
        q_sb, k_sb, v_sb, q_fx, k_fx, v_fx, lf, gates = _inproj_call(
            x, mod3, g_mix[l].reshape(1, d), w_a, w_f, w_g, b_f, b_gate[l].reshape(1, 2 * d), tm)

        y_sb = _sb_call(_heads_t(q_sb), _heads(k_sb), _value_blocks(v_sb, tq), tq)

        _, hi, mid, lo = _cumsum_call(lf.transpose(0, 2, 1))
        one = jnp.ones((b, N_HEADS, 3, s), BF16)
        cq = jnp.stack([hi, mid, lo], axis=2).astype(BF16)
        zq = jnp.zeros((b, N_HEADS, FOX_K - HEAD_DIM - 6, s), BF16)
        qt_fx = jnp.concatenate([_heads_t(q_fx), one, cq, zq], axis=2)
        k_aug = jnp.concatenate([-cq, one, zq], axis=2).transpose(0, 1, 3, 2)
        k_fx_aug = jnp.concatenate([_heads(k_fx), k_aug], axis=3)
        y_fx = _fox_call(qt_fx, k_fx_aug, _value_blocks(v_fx, tq), tq)

        x1, h2 = _merge_call(x, _unheads_t(y_sb), _unheads_t(y_fx), gates, mod3,
                             g_ffn[l].reshape(1, d), w_branch_sb[l].astype(BF16),
                             w_branch_fox[l].astype(BF16), w_out[l].astype(BF16), tm)
        x = _ffn_call(x1, h2, mod3, g_final.reshape(1, d), w_ffn_gate[l].astype(BF16),
                      w_ffn_up[l].astype(BF16), w_ffn_down[l].astype(BF16), tm)
    return x
```

```python
import functools
import math

import jax
import jax.numpy as jnp
from jax import lax
from jax.experimental import pallas as pl
from jax.experimental.pallas import tpu as pltpu

F32 = jnp.float32
BF16 = jnp.bfloat16

HEAD_DIM = 64
N_HEADS = 8
D_HEADS = N_HEADS * HEAD_DIM
N_MOD = 6
RMS_EPS = 1e-6
LOG2E = math.log2(math.e)
Q_SCALE = LOG2E * HEAD_DIM ** -0.5
FOX_K = 128
NEG_BIG = -1e30
SB_DEAD_LOG2 = 160.0
V7X_VMEM_LIMIT = 56 * 1024 * 1024
HEADS_PER_STEP = 8


def _rms_modulate(x, g, scale, shift):
    ms = jnp.mean(x * x, axis=-1, keepdims=True)
    y = x * lax.rsqrt(ms + RMS_EPS) * g
    return y * (1.0 + scale) + shift


def _const_spec(shape):
    return pl.BlockSpec(shape, lambda *_: (0,) * len(shape), pipeline_mode=pl.Buffered(1))


def _mod_kernel(c_ref, w_ref, b_ref, o_ref):
    c = c_ref[...]
    ca = c * jax.nn.sigmoid(c)
    o_ref[...] = jnp.dot(ca, w_ref[...], preferred_element_type=F32,
                         precision=lax.Precision.HIGHEST) + b_ref[...]


def _mod_call(c, w_ada, b_ada):
    b, d = c.shape
    n = w_ada.shape[1]
    tn = 1024
    return pl.pallas_call(
        _mod_kernel,
        grid=(n // tn,),
        in_specs=[pl.BlockSpec((b, d), lambda j: (0, 0)),
                  pl.BlockSpec((d, tn), lambda j: (0, j)),
                  pl.BlockSpec((1, tn), lambda j: (0, j))],
        out_specs=pl.BlockSpec((b, tn), lambda j: (0, j)),
        out_shape=jax.ShapeDtypeStruct((b, n), F32),
        name="mod",
    )(c, w_ada, b_ada.reshape(1, n))


def _inproj_kernel(x_ref, mod_ref, g_ref, wa_ref, wf_ref, wg_ref, bf_ref, bg_ref,
                   qs_ref, ks_ref, vs_ref, qf_ref, kf_ref, vf_ref, lf_ref, gate_ref):
    mod = mod_ref[...]
    h = _rms_modulate(x_ref[...], g_ref[...], mod[1:2, :], mod[0:1, :]).astype(BF16)
    a = jnp.dot(h, wa_ref[...], preferred_element_type=F32)
    d = D_HEADS
    qs_ref[...] = (a[:, 0 * d:1 * d] * Q_SCALE).astype(BF16)
    ks_ref[...] = a[:, 1 * d:2 * d].astype(BF16)
    vs_ref[...] = a[:, 2 * d:3 * d].astype(BF16)
    qf_ref[...] = (a[:, 3 * d:4 * d] * Q_SCALE).astype(BF16)
    kf_ref[...] = a[:, 4 * d:5 * d].astype(BF16)
    vf_ref[...] = a[:, 5 * d:6 * d].astype(BF16)
    f = jnp.dot(h, wf_ref[...], preferred_element_type=F32) + bf_ref[...]
    lf = -(jnp.maximum(-f, 0.0) + jnp.log(1.0 + jnp.exp(-jnp.abs(f)))) * LOG2E
    lf_ref[...] = lf[:, :N_HEADS]
    gl = jnp.dot(h, wg_ref[...], preferred_element_type=F32) + bg_ref[...]
    gate_ref[...] = jax.nn.sigmoid(gl).astype(BF16)


def _inproj_call(x, mod3, g_mix, w_a, w_f, w_g, b_f, b_g, tm):
    b, s, d = x.shape
    row = lambda bi, i: (bi, i, 0)
    head_out = pl.BlockSpec((None, tm, D_HEADS), row)
    return pl.pallas_call(
        _inproj_kernel,
        grid=(b, s // tm),
        in_specs=[pl.BlockSpec((None, tm, d), row),
                  pl.BlockSpec((None, N_MOD, d), lambda bi, i: (bi, 0, 0)),
                  _const_spec((1, d)),
                  _const_spec(w_a.shape), _const_spec(w_f.shape), _const_spec(w_g.shape),
                  _const_spec(b_f.shape), _const_spec(b_g.shape)],
        out_specs=[head_out] * 6 + [pl.BlockSpec((None, tm, N_HEADS), row),
                                    pl.BlockSpec((None, tm, 2 * d), row)],
        out_shape=[jax.ShapeDtypeStruct((b, s, D_HEADS), BF16)] * 6
        + [jax.ShapeDtypeStruct((b, s, N_HEADS), F32),
           jax.ShapeDtypeStruct((b, s, 2 * d), BF16)],
        compiler_params=pltpu.CompilerParams(
            dimension_semantics=("parallel", "parallel"), vmem_limit_bytes=V7X_VMEM_LIMIT),
        name="inproj",
    )(x, mod3, g_mix, w_a, w_f, w_g, b_f, b_g)


def _split3(v):
    hi = v.astype(BF16).astype(F32)
    r = v - hi
    mid = r.astype(BF16).astype(F32)
    lo = (r - mid).astype(BF16).astype(F32)
    return hi, mid, lo


def _cumsum_kernel(lf_ref, cum_ref, hi_ref, mid_ref, lo_ref, *, chunk):
    h, s = lf_ref.shape
    j = lax.broadcasted_iota(jnp.int32, (chunk, chunk), 0)
    t = lax.broadcasted_iota(jnp.int32, (chunk, chunk), 1)
    tri = (j <= t).astype(BF16)
    carry = jnp.zeros((h, 1), F32)
    for c in range(s // chunk):
        sl = slice(c * chunk, (c + 1) * chunk)
        pieces = jnp.concatenate(_split3(lf_ref[:, sl]), axis=0).astype(BF16)
        p = jnp.dot(pieces, tri, preferred_element_type=F32)
        cum = (p[0:h] + p[h:2 * h]) + p[2 * h:3 * h] + carry
        cum_ref[:, sl] = cum
        hi, mid, lo = _split3(cum)
        hi_ref[:, sl] = hi
        mid_ref[:, sl] = mid
        lo_ref[:, sl] = lo
        carry = cum[:, chunk - 1:chunk]


def _cumsum_call(lf_t):
    b, h, s = lf_t.shape
    spec = pl.BlockSpec((None, h, s), lambda bi: (bi, 0, 0))
    return pl.pallas_call(
        functools.partial(_cumsum_kernel, chunk=min(256, s)),
        grid=(b,),
        in_specs=[spec],
        out_specs=[spec] * 4,
        out_shape=[jax.ShapeDtypeStruct((b, h, s), F32)] * 4,
        compiler_params=pltpu.CompilerParams(dimension_semantics=("parallel",)),
        name="cumsum",
    )(lf_t)


def _softplus2(z):
    return jnp.maximum(z, 0.0) + jnp.log2(1.0 + jnp.exp2(-jnp.abs(z)))


def _sb_kernel(qt_ref, k_ref, vt_ref, o_ref, *, tq, g):
    i = pl.program_id(2)
    rows = lax.broadcasted_iota(jnp.int32, (tq, tq), 0)
    cols = lax.broadcasted_iota(jnp.int32, (tq, tq), 1)
    suffix = (cols >= rows).astype(BF16)
    past = rows < cols

    def block(j, accs, runs, masked):
        start = pl.multiple_of(j * tq, tq)
        zs = [jnp.dot(k_ref[hh, pl.ds(start, tq), :], qt_ref[hh], preferred_element_type=F32)
              for hh in range(g)]
        cs = []
        for hh in range(g):
            sp = _softplus2(zs[hh])
            if masked:
                sp = jnp.where(past, sp, 0.0)
            cs.append(jnp.dot(suffix, sp.astype(BF16), preferred_element_type=F32))
        new_accs = []
        for hh in range(g):
            w = jnp.exp2(zs[hh] - cs[hh] - runs[hh])
            if masked:
                w = jnp.where(past, w, 0.0)
            new_accs.append(accs[hh] + jnp.dot(vt_ref[hh, j], w.astype(BF16),
                                               preferred_element_type=F32))
        new_runs = [runs[hh] + cs[hh][0:1, :] for hh in range(g)]
        return tuple(new_accs), tuple(new_runs)

    accs, runs = block(i, (jnp.zeros((HEAD_DIM, tq), F32),) * g,
                       (jnp.zeros((1, tq), F32),) * g, True)

    def cond(state):
        n, _, runs = state
        return jnp.logical_and(n < i, jnp.min(functools.reduce(jnp.minimum, runs)) < SB_DEAD_LOG2)

    def body(state):
        n, accs, runs = state
        accs, runs = block(i - 1 - n, accs, runs, False)
        return n + 1, accs, runs

    _, accs, _ = lax.while_loop(cond, body, (jnp.int32(0), accs, runs))
    for hh in range(g):
        o_ref[hh] = accs[hh].astype(o_ref.dtype)


def _attn_call(body, name, qt, k, vt, tq, g):
    b, h, kd, s = qt.shape
    nk = s // tq
    return pl.pallas_call(
        functools.partial(body, tq=tq, g=g),
        grid=(b, h // g, s // tq),
        in_specs=[pl.BlockSpec((None, g, kd, tq), lambda bi, hi, i: (bi, hi, 0, i)),
                  pl.BlockSpec((None, g, s, k.shape[-1]), lambda bi, hi, i: (bi, hi, 0, 0)),
                  pl.BlockSpec((None, g, nk, HEAD_DIM, tq), lambda bi, hi, i: (bi, hi, 0, 0, 0))],
        out_specs=pl.BlockSpec((None, g, HEAD_DIM, tq), lambda bi, hi, i: (bi, hi, 0, i)),
        out_shape=jax.ShapeDtypeStruct((b, h, HEAD_DIM, s), BF16),
        compiler_params=pltpu.CompilerParams(
            dimension_semantics=("parallel", "parallel", "arbitrary"),
            vmem_limit_bytes=V7X_VMEM_LIMIT),
        name=name,
    )(qt, k, vt)


def _fox_kernel(qt_ref, k_ref, vt_ref, o_ref, *, tq, g):
    i = pl.program_id(2)
    rows = lax.broadcasted_iota(jnp.int32, (tq, tq), 0)
    cols = lax.broadcasted_iota(jnp.int32, (tq, tq), 1)
    causal = rows <= cols

    def block(j, state, masked):
        start = pl.multiple_of(j * tq, tq)
        ss = [jnp.dot(k_ref[hh, pl.ds(start, tq), :], qt_ref[hh], preferred_element_type=F32)
              for hh in range(g)]
        stats, ps = [], []
        for hh in range(g):
            m, l, _ = state[hh]
            s = ss[hh]
            if masked:
                s = jnp.where(causal, s, NEG_BIG)
            m_new = jnp.maximum(m, jnp.max(s, axis=0, keepdims=True))
            alpha = jnp.exp2(m - m_new)
            p = jnp.exp2(s - m_new)
            stats.append((m_new, alpha, alpha * l + jnp.sum(p, axis=0, keepdims=True)))
            ps.append(p.astype(BF16))
        out = []
        for hh in range(g):
            m_new, alpha, l = stats[hh]
            acc = alpha * state[hh][2] + jnp.dot(vt_ref[hh, j], ps[hh],
                                                 preferred_element_type=F32)
            out.append((m_new, l, acc))
        return tuple(out)

    state = ((jnp.full((1, tq), NEG_BIG, F32), jnp.zeros((1, tq), F32),
              jnp.zeros((HEAD_DIM, tq), F32)),) * g
    state = block(i, state, True)
    state = lax.fori_loop(0, i, lambda j, st: block(j, st, False), state)
    for hh in range(g):
        _, l, acc = state[hh]
        o_ref[hh] = (acc / l).astype(o_ref.dtype)


def _merge_kernel(x_ref, ys_ref, yf_ref, gate_ref, mod_ref, g_ref, ws_ref, wfx_ref, wo_ref,
                  x1_ref, h2_ref):
    d = x_ref.shape[-1]
    mod = mod_ref[...]
    a = jnp.dot(ys_ref[...], ws_ref[...], preferred_element_type=F32)
    b = jnp.dot(yf_ref[...], wfx_ref[...], preferred_element_type=F32)
    gate = gate_ref[...].astype(F32)
    merged = gate[:, :d] * a + gate[:, d:] * b
    x1 = x_ref[...] + mod[2:3, :] * jnp.dot(merged.astype(BF16), wo_ref[...],
                                            preferred_element_type=F32)
    x1_ref[...] = x1
    h2_ref[...] = _rms_modulate(x1, g_ref[...], mod[4:5, :], mod[3:4, :]).astype(BF16)


def _merge_call(x, y_sb, y_fx, gates, mod3, g_ffn, w_bs, w_bf, w_out, tm):
    b, s, d = x.shape
    row = lambda bi, i: (bi, i, 0)
    return pl.pallas_call(
        _merge_kernel,
        grid=(b, s // tm),
        in_specs=[pl.BlockSpec((None, tm, d), row),
                  pl.BlockSpec((None, tm, D_HEADS), row),
                  pl.BlockSpec((None, tm, D_HEADS), row),
                  pl.BlockSpec((None, tm, 2 * d), row),
                  pl.BlockSpec((None, N_MOD, d), lambda bi, i: (bi, 0, 0)),
                  _const_spec((1, d)),
                  _const_spec(w_bs.shape), _const_spec(w_bf.shape), _const_spec(w_out.shape)],
        out_specs=[pl.BlockSpec((None, tm, d), row), pl.BlockSpec((None, tm, d), row)],
        out_shape=[jax.ShapeDtypeStruct((b, s, d), F32), jax.ShapeDtypeStruct((b, s, d), BF16)],
        compiler_params=pltpu.CompilerParams(
            dimension_semantics=("parallel", "parallel"), vmem_limit_bytes=V7X_VMEM_LIMIT),
        name="merge",
    )(x, y_sb, y_fx, gates, mod3, g_ffn, w_bs, w_bf, w_out)


def _ffn_kernel(x1_ref, h2_ref, mod_ref, g_ref, wg_ref, wu_ref, wd_ref, o_ref, *, n_chunks):
    h2 = h2_ref[...]
    dff = wg_ref.shape[1]
    cw = dff // n_chunks
    ffn = None
    for c in range(n_chunks):
        sl = slice(c * cw, (c + 1) * cw)
        gt = jnp.dot(h2, wg_ref[:, sl], preferred_element_type=F32)
        up = jnp.dot(h2, wu_ref[:, sl], preferred_element_type=F32)
        act = (gt * jax.nn.sigmoid(gt) * up).astype(BF16)
        part = jnp.dot(act, wd_ref[sl, :], preferred_element_type=F32)
        ffn = part if ffn is None else ffn + part
    x2 = x1_ref[...] + mod_ref[5:6, :] * ffn
    ms = jnp.mean(x2 * x2, axis=-1, keepdims=True)
    o_ref[...] = x2 * lax.rsqrt(ms + RMS_EPS) * g_ref[...]


def _ffn_call(x1, h2, mod3, g_final, w_gate, w_up, w_down, tm):
    b, s, d = x1.shape
    row = lambda bi, i: (bi, i, 0)
    dff = w_gate.shape[1]
    n_chunks = 2 if dff % 256 == 0 else 1
    return pl.pallas_call(
        functools.partial(_ffn_kernel, n_chunks=n_chunks),
        grid=(b, s // tm),
        in_specs=[pl.BlockSpec((None, tm, d), row),
                  pl.BlockSpec((None, tm, d), row),
                  pl.BlockSpec((None, N_MOD, d), lambda bi, i: (bi, 0, 0)),
                  _const_spec((1, d)),
                  _const_spec(w_gate.shape), _const_spec(w_up.shape), _const_spec(w_down.shape)],
        out_specs=pl.BlockSpec((None, tm, d), row),
        out_shape=jax.ShapeDtypeStruct((b, s, d), F32),
        compiler_params=pltpu.CompilerParams(
            dimension_semantics=("parallel", "parallel"), vmem_limit_bytes=V7X_VMEM_LIMIT),
        name="ffn",
    )(x1, h2, mod3, g_final, w_gate, w_up, w_down)


def _heads_t(t):
    b, s, _ = t.shape
    return t.reshape(b, s, N_HEADS, HEAD_DIM).transpose(0, 2, 3, 1)


def _heads(t):
    b, s, _ = t.shape
    return t.reshape(b, s, N_HEADS, HEAD_DIM).transpose(0, 2, 1, 3)


def _value_blocks(t, tk):
    b, s, _ = t.shape
    return t.reshape(b, s // tk, tk, N_HEADS, HEAD_DIM).transpose(0, 3, 1, 4, 2)


def _unheads_t(t):
    b, h, dh, s = t.shape
    return t.transpose(0, 3, 1, 2).reshape(b, s, h * dh)


def kernel(x, c, w_ada, b_ada, g_mix, w_in, b_forget, b_gate, w_branch_sb, w_branch_fox,
           w_out, g_ffn, w_ffn_gate, w_ffn_up, w_ffn_down, g_final):
    b, s, d = x.shape
    depth = w_ada.shape[0]
    assert depth == 1, "the final RMSNorm is fused into the last (only) layer's FFN call"
    tm = min(512, s)
    tq = min(256, s)
    n_qkv = 6 * D_HEADS
    for l in range(depth):
        mod3 = _mod_call(c, w_ada[l], b_ada[l]).reshape(b, N_MOD, d)
        w_a = w_in[l, :, :n_qkv].astype(BF16)
        w_f = jnp.pad(w_in[l, :, n_qkv:n_qkv + N_HEADS], ((0, 0), (0, 128 - N_HEADS))).astype(BF16)
        w_g = w_in[l, :, n_qkv + N_HEADS:].astype(BF16)
        b_f = jnp.pad(b_forget[l], (0, 128 - N_HEADS)).reshape(1, 128)
        q_sb, k_sb, v_sb, q_fx, k_fx, v_fx, lf, gates = _inproj_call(
            x, mod3, g_mix[l].reshape(1, d), w_a, w_f, w_g, b_f, b_gate[l].reshape(1, 2 * d), tm)

        y_sb = _attn_call(_sb_kernel, "sb_attn", _heads_t(q_sb), _heads(k_sb),
                          _value_blocks(v_sb, tq), tq, HEADS_PER_STEP)

        _, hi, mid, lo = _cumsum_call(lf.transpose(0, 2, 1))
        one = jnp.ones((b, N_HEADS, 3, s), BF16)
        cq = jnp.stack([hi, mid, lo], axis=2).astype(BF16)
        zq = jnp.zeros((b, N_HEADS, FOX_K - HEAD_DIM - 6, s), BF16)
        qt_fx = jnp.concatenate([_heads_t(q_fx), one, cq, zq], axis=2)
        k_aug = jnp.concatenate([-cq, one, zq], axis=2).transpose(0, 1, 3, 2)
        k_fx_aug = jnp.concatenate([_heads(k_fx), k_aug], axis=3)
        y_fx = _attn_call(_fox_kernel, "fox_attn", qt_fx, k_fx_aug, _value_blocks(v_fx, tq), tq,
                          HEADS_PER_STEP)

        x1, h2 = _merge_call(x, _unheads_t(y_sb), _unheads_t(y_fx), gates, mod3,
                             g_ffn[l].reshape(1, d), w_branch_sb[l].astype(BF16),
                             w_branch_fox[l].astype(BF16), w_out[l].astype(BF16), tm)
        x = _ffn_call(x1, h2, mod3, g_final.reshape(1, d), w_ffn_gate[l].astype(BF16),
                      w_ffn_up[l].astype(BF16), w_ffn_down[l].astype(BF16), tm)
    return x
```

```python
import functools
import math

import jax
import jax.numpy as jnp
from jax import lax
from jax.experimental import pallas as pl
from jax.experimental.pallas import tpu as pltpu

F32 = jnp.float32
BF16 = jnp.bfloat16

HEAD_DIM = 64
N_HEADS = 8
D_HEADS = N_HEADS * HEAD_DIM
PAIR = 2 * HEAD_DIM
N_MOD = 6
RMS_EPS = 1e-6
LOG2E = math.log2(math.e)
Q_SCALE = LOG2E * HEAD_DIM ** -0.5
AUG = 16
NEG_BIG = -1e30
SB_DEAD_LOG2 = 160.0
V7X_VMEM_LIMIT = 56 * 1024 * 1024
NT_DIMS = (((1,), (1,)), ((), ()))


def _rms_modulate(x, g, scale, shift):
    ms = jnp.mean(x * x, axis=-1, keepdims=True)
    y = x * lax.rsqrt(ms + RMS_EPS) * g
    return y * (1.0 + scale) + shift


def _const_spec(shape):
    return pl.BlockSpec(shape, lambda *_: (0,) * len(shape), pipeline_mode=pl.Buffered(1))


def _mod_kernel(c_ref, w_ref, b_ref, o_ref):
    c = c_ref[...]
    ca = c * jax.nn.sigmoid(c)
    o_ref[...] = jnp.dot(ca, w_ref[...], preferred_element_type=F32,
                         precision=lax.Precision.HIGHEST) + b_ref[...]


def _mod_call(c, w_ada, b_ada):
    b, d = c.shape
    n = w_ada.shape[1]
    tn = 1024
    return pl.pallas_call(
        _mod_kernel,
        grid=(n // tn,),
        in_specs=[pl.BlockSpec((b, d), lambda j: (0, 0)),
                  pl.BlockSpec((d, tn), lambda j: (0, j)),
                  pl.BlockSpec((1, tn), lambda j: (0, j))],
        out_specs=pl.BlockSpec((b, tn), lambda j: (0, j)),
        out_shape=jax.ShapeDtypeStruct((b, n), F32),
        name="mod",
    )(c, w_ada, b_ada.reshape(1, n))


def _inproj_kernel(x_ref, mod_ref, g_ref, wk_ref, wqvt_ref, wf_ref, wg_ref, bf_ref, bg_ref,
                   ks_ref, kf_ref, qts_ref, qtf_ref, vts_ref, vtf_ref, lf_ref, gate_ref, *, tk):
    mod = mod_ref[...]
    h = _rms_modulate(x_ref[...], g_ref[...], mod[1:2, :], mod[0:1, :]).astype(BF16)
    d = D_HEADS
    kk = jnp.dot(h, wk_ref[...], preferred_element_type=F32)
    ks_ref[...] = kk[:, :d].astype(BF16)
    kf_ref[...] = kk[:, d:].astype(BF16)
    t = lax.dot_general(wqvt_ref[...], h, NT_DIMS, preferred_element_type=F32)
    qts_ref[...] = (t[0 * d:1 * d] * Q_SCALE).astype(BF16)
    qtf_ref[...] = (t[1 * d:2 * d] * Q_SCALE).astype(BF16)
    for c in range(vts_ref.shape[0]):
        vts_ref[c] = t[2 * d:3 * d, c * tk:(c + 1) * tk].astype(BF16)
        vtf_ref[c] = t[3 * d:4 * d, c * tk:(c + 1) * tk].astype(BF16)
    f = jnp.dot(h, wf_ref[...], preferred_element_type=F32) + bf_ref[...]
    lf = -(jnp.maximum(-f, 0.0) + jnp.log(1.0 + jnp.exp(-jnp.abs(f)))) * LOG2E
    lf_ref[...] = lf[:, :N_HEADS]
    gl = jnp.dot(h, wg_ref[...], preferred_element_type=F32) + bg_ref[...]
    gate_ref[...] = jax.nn.sigmoid(gl).astype(BF16)


def _inproj_call(x, mod3, g_mix, w_k, w_qvt, w_f, w_g, b_f, b_g, tm, tk):
    b, s, d = x.shape
    row = lambda bi, i: (bi, i, 0)
    col = lambda bi, i: (bi, 0, i)
    k_spec = pl.BlockSpec((None, tm, D_HEADS), row)
    qt_spec = pl.BlockSpec((None, D_HEADS, tm), col)
    vt_spec = pl.BlockSpec((None, tm // tk, D_HEADS, tk), lambda bi, i: (bi, i, 0, 0))
    return pl.pallas_call(
        functools.partial(_inproj_kernel, tk=tk),
        grid=(b, s // tm),
        in_specs=[pl.BlockSpec((None, tm, d), row),
                  pl.BlockSpec((None, N_MOD, d), lambda bi, i: (bi, 0, 0)),
                  _const_spec((1, d)),
                  _const_spec(w_k.shape), _const_spec(w_qvt.shape), _const_spec(w_f.shape),
                  _const_spec(w_g.shape), _const_spec(b_f.shape), _const_spec(b_g.shape)],
        out_specs=[k_spec, k_spec, qt_spec, qt_spec, vt_spec, vt_spec,
                   pl.BlockSpec((None, tm, N_HEADS), row),
                   pl.BlockSpec((None, tm, 2 * d), row)],
        out_shape=[jax.ShapeDtypeStruct((b, s, D_HEADS), BF16)] * 2
        + [jax.ShapeDtypeStruct((b, D_HEADS, s), BF16)] * 2
        + [jax.ShapeDtypeStruct((b, s // tk, D_HEADS, tk), BF16)] * 2
        + [jax.ShapeDtypeStruct((b, s, N_HEADS), F32),
           jax.ShapeDtypeStruct((b, s, 2 * d), BF16)],
        compiler_params=pltpu.CompilerParams(
            dimension_semantics=("parallel", "parallel"), vmem_limit_bytes=V7X_VMEM_LIMIT),
        name="inproj",
    )(x, mod3, g_mix, w_k, w_qvt, w_f, w_g, b_f, b_g)


def _split3(v):
    hi = v.astype(BF16).astype(F32)
    r = v - hi
    mid = r.astype(BF16).astype(F32)
    lo = (r - mid).astype(BF16).astype(F32)
    return hi, mid, lo


def _aug_lane(h):
    return (h // 2) * PAIR + (h % 2) * AUG


def _cumsum_kernel(lf_ref, lft_ref, augk_ref, augqt_ref, *, chunk):
    s, h = lf_ref.shape
    n3 = 3 * h
    r = lax.broadcasted_iota(jnp.int32, (chunk, chunk), 0)
    c = lax.broadcasted_iota(jnp.int32, (chunk, chunk), 1)
    lower = (c <= r).astype(BF16)
    upper = (r <= c).astype(BF16)

    pr = lax.broadcasted_iota(jnp.int32, (n3, augk_ref.shape[1]), 0)
    pc = lax.broadcasted_iota(jnp.int32, (n3, augk_ref.shape[1]), 1)
    place_k = jnp.zeros((n3, augk_ref.shape[1]), F32)
    ones_k = jnp.zeros((1, augk_ref.shape[1]), F32)
    lane = lax.broadcasted_iota(jnp.int32, (1, augk_ref.shape[1]), 1)
    qr = lax.broadcasted_iota(jnp.int32, (h * AUG, n3), 0)
    qc = lax.broadcasted_iota(jnp.int32, (h * AUG, n3), 1)
    place_q = jnp.zeros((h * AUG, n3), F32)
    ones_q = jnp.zeros((h * AUG, 1), F32)
    qrow = lax.broadcasted_iota(jnp.int32, (h * AUG, 1), 0)
    for hh in range(h):
        for p in range(3):
            place_k = jnp.where((pr == p * h + hh) & (pc == _aug_lane(hh) + p), -1.0, place_k)
            ones_k = jnp.where(lane == _aug_lane(hh) + 3 + p, 1.0, ones_k)
            place_q = jnp.where((qr == hh * AUG + 3 + p) & (qc == p * h + hh), 1.0, place_q)
            ones_q = jnp.where(qrow == hh * AUG + p, 1.0, ones_q)
    place_k = place_k.astype(BF16)
    place_q = place_q.astype(BF16)

    carry_col = jnp.zeros((1, h), F32)
    carry_row = jnp.zeros((h, 1), F32)
    for ci in range(s // chunk):
        sl = slice(ci * chunk, (ci + 1) * chunk)
        pieces = jnp.concatenate(_split3(lf_ref[sl, :]), axis=1).astype(BF16)
        p3 = jnp.dot(lower, pieces, preferred_element_type=F32)
        cum = (p3[:, 0:h] + p3[:, h:2 * h]) + p3[:, 2 * h:n3] + carry_col
        carry_col = cum[chunk - 1:chunk, :]
        stacked = jnp.concatenate(_split3(cum), axis=1).astype(BF16)
        augk_ref[sl, :] = (jnp.dot(stacked, place_k, preferred_element_type=F32)
                           + ones_k).astype(BF16)
        pieces = jnp.concatenate(_split3(lft_ref[:, sl]), axis=0).astype(BF16)
        p3 = jnp.dot(pieces, upper, preferred_element_type=F32)
        cum = (p3[0:h] + p3[h:2 * h]) + p3[2 * h:n3] + carry_row
        carry_row = cum[:, chunk - 1:chunk]
        stacked = jnp.concatenate(_split3(cum), axis=0).astype(BF16)
        augqt_ref[:, sl] = (jnp.dot(place_q, stacked, preferred_element_type=F32)
                            + ones_q).astype(BF16)


def _cumsum_call(lf, lf_t):
    b, s, h = lf.shape
    return pl.pallas_call(
        functools.partial(_cumsum_kernel, chunk=min(256, s)),
        grid=(b,),
        in_specs=[pl.BlockSpec((None, s, h), lambda bi: (bi, 0, 0)),
                  pl.BlockSpec((None, h, s), lambda bi: (bi, 0, 0))],
        out_specs=[pl.BlockSpec((None, s, D_HEADS), lambda bi: (bi, 0, 0)),
                   pl.BlockSpec((None, h * AUG, s), lambda bi: (bi, 0, 0))],
        out_shape=[jax.ShapeDtypeStruct((b, s, D_HEADS), BF16),
                   jax.ShapeDtypeStruct((b, h * AUG, s), BF16)],
        compiler_params=pltpu.CompilerParams(dimension_semantics=("parallel",)),
        name="cumsum",
    )(lf, lf_t)


def _softplus2(z):
    return jnp.maximum(z, 0.0) + jnp.log2(1.0 + jnp.exp2(-jnp.abs(z)))


def _fill_query_operand(rhs_ref, qt_ref, augqt_ref):
    rhs_ref[...] = jnp.zeros(rhs_ref.shape, rhs_ref.dtype)
    for hh in range(N_HEADS):
        half = hh % 2
        rhs_ref[hh, half * HEAD_DIM:(half + 1) * HEAD_DIM, :] = (
            qt_ref[hh * HEAD_DIM:(hh + 1) * HEAD_DIM, :])
        if augqt_ref is not None:
            rhs_ref[hh, PAIR + half * AUG:PAIR + (half + 1) * AUG, :] = (
                augqt_ref[hh * AUG:(hh + 1) * AUG, :])


def _sb_kernel(qt_ref, k_ref, vt_ref, o_ref, rhs_ref, *, tq):
    g = N_HEADS
    i = pl.program_id(1)
    _fill_query_operand(rhs_ref, qt_ref, None)
    rows = lax.broadcasted_iota(jnp.int32, (tq, tq), 0)
    cols = lax.broadcasted_iota(jnp.int32, (tq, tq), 1)
    suffix = (cols >= rows).astype(BF16)
    past = rows < cols

    def block(j, accs, runs, masked):
        start = pl.multiple_of(j * tq, tq)
        zs = [jnp.dot(k_ref[pl.ds(start, tq), (hh // 2) * PAIR:(hh // 2 + 1) * PAIR], rhs_ref[hh],
                      preferred_element_type=F32) for hh in range(g)]
        cs = []
        for hh in range(g):
            sp = _softplus2(zs[hh])
            if masked:
                sp = jnp.where(past, sp, 0.0)
            cs.append(jnp.dot(suffix, sp.astype(BF16), preferred_element_type=F32))
        new_accs = []
        for hh in range(g):
            w = jnp.exp2(zs[hh] - cs[hh] - runs[hh])
            if masked:
                w = jnp.where(past, w, 0.0)
            vt = vt_ref[j, hh * HEAD_DIM:(hh + 1) * HEAD_DIM, :]
            new_accs.append(accs[hh] + jnp.dot(vt, w.astype(BF16), preferred_element_type=F32))
        new_runs = [runs[hh] + cs[hh][0:1, :] for hh in range(g)]
        return tuple(new_accs), tuple(new_runs)

    accs, runs = block(i, (jnp.zeros((HEAD_DIM, tq), F32),) * g,
                       (jnp.zeros((1, tq), F32),) * g, True)

    def cond(state):
        n, _, runs = state
        return jnp.logical_and(n < i, jnp.min(functools.reduce(jnp.minimum, runs)) < SB_DEAD_LOG2)

    def body(state):
        n, accs, runs = state
        accs, runs = block(i - 1 - n, accs, runs, False)
        return n + 1, accs, runs

    _, accs, _ = lax.while_loop(cond, body, (jnp.int32(0), accs, runs))
    for hh in range(g):
        o_ref[hh * HEAD_DIM:(hh + 1) * HEAD_DIM, :] = accs[hh].astype(o_ref.dtype)


def _fox_kernel(qt_ref, augqt_ref, k_ref, augk_ref, vt_ref, o_ref, rhs_ref, *, tq):
    g = N_HEADS
    i = pl.program_id(1)
    _fill_query_operand(rhs_ref, qt_ref, augqt_ref)
    rows = lax.broadcasted_iota(jnp.int32, (tq, tq), 0)
    cols = lax.broadcasted_iota(jnp.int32, (tq, tq), 1)
    causal = rows <= cols

    def block(j, state, masked):
        start = pl.multiple_of(j * tq, tq)
        ss = []
        for hh in range(g):
            pair = slice((hh // 2) * PAIR, (hh // 2 + 1) * PAIR)
            lhs = jnp.concatenate([k_ref[pl.ds(start, tq), pair], augk_ref[pl.ds(start, tq), pair]],
                                  axis=1)
            ss.append(jnp.dot(lhs, rhs_ref[hh], preferred_element_type=F32))
        stats, ps = [], []
        for hh in range(g):
            m, l, _ = state[hh]
            s = ss[hh]
            if masked:
                s = jnp.where(causal, s, NEG_BIG)
            m_new = jnp.maximum(m, jnp.max(s, axis=0, keepdims=True))
            alpha = jnp.exp2(m - m_new)
            p = jnp.exp2(s - m_new)
            stats.append((m_new, alpha, alpha * l + jnp.sum(p, axis=0, keepdims=True)))
            ps.append(p.astype(BF16))
        out = []
        for hh in range(g):
            m_new, alpha, l = stats[hh]
            vt = vt_ref[j, hh * HEAD_DIM:(hh + 1) * HEAD_DIM, :]
            acc = alpha * state[hh][2] + jnp.dot(vt, ps[hh], preferred_element_type=F32)
            out.append((m_new, l, acc))
        return tuple(out)

    state = ((jnp.full((1, tq), NEG_BIG, F32), jnp.zeros((1, tq), F32),
              jnp.zeros((HEAD_DIM, tq), F32)),) * g
    state = block(i, state, True)
    state = lax.fori_loop(0, i, lambda j, st: block(j, st, False), state)
    for hh in range(g):
        _, l, acc = state[hh]
        o_ref[hh * HEAD_DIM:(hh + 1) * HEAD_DIM, :] = (acc / l).astype(o_ref.dtype)


def _attn_call(qt, k, vt, tq, aug=None):
    b, d, s = qt.shape
    nk = s // tq
    q_spec = pl.BlockSpec((None, d, tq), lambda bi, i: (bi, 0, i))
    k_spec = pl.BlockSpec((None, s, d), lambda bi, i: (bi, 0, 0))
    vt_spec = pl.BlockSpec((None, nk, d, tq), lambda bi, i: (bi, 0, 0, 0))
    if aug is None:
        body, name, kdim = _sb_kernel, "sb_attn", PAIR
        operands, in_specs = (qt, k, vt), [q_spec, k_spec, vt_spec]
    else:
        augk, augqt = aug
        body, name, kdim = _fox_kernel, "fox_attn", 2 * PAIR
        operands = (qt, augqt, k, augk, vt)
        in_specs = [q_spec, pl.BlockSpec((None, N_HEADS * AUG, tq), lambda bi, i: (bi, 0, i)),
                    k_spec, k_spec, vt_spec]
    return pl.pallas_call(
        functools.partial(body, tq=tq),
        grid=(b, s // tq),
        in_specs=in_specs,
        out_specs=q_spec,
        out_shape=jax.ShapeDtypeStruct((b, d, s), BF16),
        scratch_shapes=[pltpu.VMEM((N_HEADS, kdim, tq), BF16)],
        compiler_params=pltpu.CompilerParams(
            dimension_semantics=("parallel", "arbitrary"), vmem_limit_bytes=V7X_VMEM_LIMIT),
        name=name,
    )(*operands)


def _merge_kernel(x_ref, yst_ref, yft_ref, gate_ref, mod_ref, g_ref, ws_ref, wfx_ref, wo_ref,
                  x1_ref, h2_ref):
    d = x_ref.shape[-1]
    mod = mod_ref[...]
    a = jnp.dot(yst_ref[...].T, ws_ref[...], preferred_element_type=F32)
    b = jnp.dot(yft_ref[...].T, wfx_ref[...], preferred_element_type=F32)
    gate = gate_ref[...].astype(F32)
    merged = gate[:, :d] * a + gate[:, d:] * b
    x1 = x_ref[...] + mod[2:3, :] * jnp.dot(merged.astype(BF16), wo_ref[...],
                                            preferred_element_type=F32)
    x1_ref[...] = x1
    h2_ref[...] = _rms_modulate(x1, g_ref[...], mod[4:5, :], mod[3:4, :]).astype(BF16)


def _merge_call(x, yt_sb, yt_fx, gates, mod3, g_ffn, w_bs, w_bf, w_out, tm):
    b, s, d = x.shape
    row = lambda bi, i: (bi, i, 0)
    yt_spec = pl.BlockSpec((None, D_HEADS, tm), lambda bi, i: (bi, 0, i))
    return pl.pallas_call(
        _merge_kernel,
        grid=(b, s // tm),
        in_specs=[pl.BlockSpec((None, tm, d), row), yt_spec, yt_spec,
                  pl.BlockSpec((None, tm, 2 * d), row),
                  pl.BlockSpec((None, N_MOD, d), lambda bi, i: (bi, 0, 0)),
                  _const_spec((1, d)),
                  _const_spec(w_bs.shape), _const_spec(w_bf.shape), _const_spec(w_out.shape)],
        out_specs=[pl.BlockSpec((None, tm, d), row), pl.BlockSpec((None, tm, d), row)],
        out_shape=[jax.ShapeDtypeStruct((b, s, d), F32), jax.ShapeDtypeStruct((b, s, d), BF16)],
        compiler_params=pltpu.CompilerParams(
            dimension_semantics=("parallel", "parallel"), vmem_limit_bytes=V7X_VMEM_LIMIT),
        name="merge",
    )(x, yt_sb, yt_fx, gates, mod3, g_ffn, w_bs, w_bf, w_out)


def _ffn_kernel(x1_ref, h2_ref, mod_ref, g_ref, wg_ref, wu_ref, wd_ref, o_ref, *, n_chunks):
    h2 = h2_ref[...]
    dff = wg_ref.shape[1]
    cw = dff // n_chunks
    ffn = None
    for c in range(n_chunks):
        sl = slice(c * cw, (c + 1) * cw)
        gt = jnp.dot(h2, wg_ref[:, sl], preferred_element_type=F32)
        up = jnp.dot(h2, wu_ref[:, sl], preferred_element_type=F32)
        act = (gt * jax.nn.sigmoid(gt) * up).astype(BF16)
        part = jnp.dot(act, wd_ref[sl, :], preferred_element_type=F32)
        ffn = part if ffn is None else ffn + part
    x2 = x1_ref[...] + mod_ref[5:6, :] * ffn
    ms = jnp.mean(x2 * x2, axis=-1, keepdims=True)
    o_ref[...] = x2 * lax.rsqrt(ms + RMS_EPS) * g_ref[...]


def _ffn_call(x1, h2, mod3, g_final, w_gate, w_up, w_down, tm):
    b, s, d = x1.shape
    row = lambda bi, i: (bi, i, 0)
    dff = w_gate.shape[1]
    n_chunks = 2 if dff % 256 == 0 else 1
    return pl.pallas_call(
        functools.partial(_ffn_kernel, n_chunks=n_chunks),
        grid=(b, s // tm),
        in_specs=[pl.BlockSpec((None, tm, d), row),
                  pl.BlockSpec((None, tm, d), row),
                  pl.BlockSpec((None, N_MOD, d), lambda bi, i: (bi, 0, 0)),
                  _const_spec((1, d)),
                  _const_spec(w_gate.shape), _const_spec(w_up.shape), _const_spec(w_down.shape)],
        out_specs=pl.BlockSpec((None, tm, d), row),
        out_shape=jax.ShapeDtypeStruct((b, s, d), F32),
        compiler_params=pltpu.CompilerParams(
            dimension_semantics=("parallel", "parallel"), vmem_limit_bytes=V7X_VMEM_LIMIT),
        name="ffn",
    )(x1, h2, mod3, g_final, w_gate, w_up, w_down)


def kernel(x, c, w_ada, b_ada, g_mix, w_in, b_forget, b_gate, w_branch_sb, w_branch_fox,
           w_out, g_ffn, w_ffn_gate, w_ffn_up, w_ffn_down, g_final):
    b, s, d = x.shape
    depth = w_ada.shape[0]
    assert depth == 1, "the final RMSNorm is fused into the last (only) layer's FFN call"
    tm = min(512, s)
    tq = min(256, s)
    dh = D_HEADS
    for l in range(depth):
        mod3 = _mod_call(c, w_ada[l], b_ada[l]).reshape(b, N_MOD, d)
        w = w_in[l]
        w_k = jnp.concatenate([w[:, dh:2 * dh], w[:, 4 * dh:5 * dh]], axis=1).astype(BF16)
        w_qvt = jnp.concatenate([w[:, 0:dh], w[:, 3 * dh:4 * dh], w[:, 2 * dh:3 * dh],
                                 w[:, 5 * dh:6 * dh]], axis=1).T.astype(BF16)
        w_f = jnp.pad(w[:, 6 * dh:6 * dh + N_HEADS], ((0, 0), (0, 128 - N_HEADS))).astype(BF16)
        w_g = w[:, 6 * dh + N_HEADS:].astype(BF16)
        b_f = jnp.pad(b_forget[l], (0, 128 - N_HEADS)).reshape(1, 128)
        k_sb, k_fx, qt_sb, qt_fx, vt_sb, vt_fx, lf, gates = _inproj_call(
            x, mod3, g_mix[l].reshape(1, d), w_k, w_qvt, w_f, w_g, b_f,
            b_gate[l].reshape(1, 2 * d), tm, tq)

        yt_sb = _attn_call(qt_sb, k_sb, vt_sb, tq)
        aug = _cumsum_call(lf, lf.transpose(0, 2, 1))
        yt_fx = _attn_call(qt_fx, k_fx, vt_fx, tq, aug)

        x1, h2 = _merge_call(x, yt_sb, yt_fx, gates, mod3, g_ffn[l].reshape(1, d),
                             w_branch_sb[l].astype(BF16), w_branch_fox[l].astype(BF16),
                             w_out[l].astype(BF16), tm)
        x = _ffn_call(x1, h2, mod3, g_final.reshape(1, d), w_ffn_gate[l].astype(BF16),
                      w_ffn_up[l].astype(BF16), w_ffn_down[l].astype(BF16), tm)
    return x
```

```python
import functools
import math

import jax
import jax.numpy as jnp
from jax import lax
from jax.experimental import pallas as pl
from jax.experimental.pallas import tpu as pltpu

F32 = jnp.float32
BF16 = jnp.bfloat16

HEAD_DIM = 64
N_HEADS = 8
D_HEADS = N_HEADS * HEAD_DIM
PAIR = 2 * HEAD_DIM
N_MOD = 6
RMS_EPS = 1e-6
LOG2E = math.log2(math.e)
Q_SCALE = LOG2E * HEAD_DIM ** -0.5
AUG = 16
DEN_ROWS = 16
NEG_BIG = -1e30
SB_DEAD_LOG2 = 160.0
V7X_VMEM_LIMIT = 56 * 1024 * 1024
NT_DIMS = (((1,), (1,)), ((), ()))


def _rms_modulate(x, g, scale, shift):
    ms = jnp.mean(x * x, axis=-1, keepdims=True)
    y = x * lax.rsqrt(ms + RMS_EPS) * g
    return y * (1.0 + scale) + shift


def _const_spec(shape):
    return pl.BlockSpec(shape, lambda *_: (0,) * len(shape), pipeline_mode=pl.Buffered(1))


def _mod_kernel(c_ref, w_ref, b_ref, o_ref):
    c = c_ref[...]
    ca = c * jax.nn.sigmoid(c)
    o_ref[...] = jnp.dot(ca, w_ref[...], preferred_element_type=F32,
                         precision=lax.Precision.HIGHEST) + b_ref[...]


def _mod_call(c, w_ada, b_ada):
    b, d = c.shape
    n = w_ada.shape[1]
    tn = 1024
    return pl.pallas_call(
        _mod_kernel,
        grid=(n // tn,),
        in_specs=[pl.BlockSpec((b, d), lambda j: (0, 0)),
                  pl.BlockSpec((d, tn), lambda j: (0, j)),
                  pl.BlockSpec((1, tn), lambda j: (0, j))],
        out_specs=pl.BlockSpec((b, tn), lambda j: (0, j)),
        out_shape=jax.ShapeDtypeStruct((b, n), F32),
        name="mod",
    )(c, w_ada, b_ada.reshape(1, n))


def _inproj_kernel(x_ref, mod_ref, g_ref, wk_ref, wqvt_ref, wf_ref, wg_ref, bf_ref, bg_ref,
                   ks_ref, kf_ref, qts_ref, qtf_ref, vts_ref, vtf_ref, lf_ref, gate_ref, *, tk):
    mod = mod_ref[...]
    h = _rms_modulate(x_ref[...], g_ref[...], mod[1:2, :], mod[0:1, :]).astype(BF16)
    d = D_HEADS
    kk = jnp.dot(h, wk_ref[...], preferred_element_type=F32)
    ks_ref[...] = kk[:, :d].astype(BF16)
    kf_ref[...] = kk[:, d:].astype(BF16)
    t = lax.dot_general(wqvt_ref[...], h, NT_DIMS, preferred_element_type=F32)
    qts_ref[...] = (t[0 * d:1 * d] * Q_SCALE).astype(BF16)
    qtf_ref[...] = (t[1 * d:2 * d] * Q_SCALE).astype(BF16)
    for c in range(vts_ref.shape[0]):
        vts_ref[c] = t[2 * d:3 * d, c * tk:(c + 1) * tk].astype(BF16)
        vtf_ref[c] = t[3 * d:4 * d, c * tk:(c + 1) * tk].astype(BF16)
    f = jnp.dot(h, wf_ref[...], preferred_element_type=F32) + bf_ref[...]
    lf = -(jnp.maximum(-f, 0.0) + jnp.log(1.0 + jnp.exp(-jnp.abs(f)))) * LOG2E
    lf_ref[...] = lf[:, :N_HEADS]
    gl = jnp.dot(h, wg_ref[...], preferred_element_type=F32) + bg_ref[...]
    gate_ref[...] = jax.nn.sigmoid(gl).astype(BF16)


def _inproj_call(x, mod3, g_mix, w_k, w_qvt, w_f, w_g, b_f, b_g, tm, tk):
    b, s, d = x.shape
    row = lambda bi, i: (bi, i, 0)
    col = lambda bi, i: (bi, 0, i)
    k_spec = pl.BlockSpec((None, tm, D_HEADS), row)
    qt_spec = pl.BlockSpec((None, D_HEADS, tm), col)
    vt_spec = pl.BlockSpec((None, tm // tk, D_HEADS, tk), lambda bi, i: (bi, i, 0, 0))
    return pl.pallas_call(
        functools.partial(_inproj_kernel, tk=tk),
        grid=(b, s // tm),
        in_specs=[pl.BlockSpec((None, tm, d), row),
                  pl.BlockSpec((None, N_MOD, d), lambda bi, i: (bi, 0, 0)),
                  _const_spec((1, d)),
                  _const_spec(w_k.shape), _const_spec(w_qvt.shape), _const_spec(w_f.shape),
                  _const_spec(w_g.shape), _const_spec(b_f.shape), _const_spec(b_g.shape)],
        out_specs=[k_spec, k_spec, qt_spec, qt_spec, vt_spec, vt_spec,
                   pl.BlockSpec((None, tm, N_HEADS), row),
                   pl.BlockSpec((None, tm, 2 * d), row)],
        out_shape=[jax.ShapeDtypeStruct((b, s, D_HEADS), BF16)] * 2
        + [jax.ShapeDtypeStruct((b, D_HEADS, s), BF16)] * 2
        + [jax.ShapeDtypeStruct((b, s // tk, D_HEADS, tk), BF16)] * 2
        + [jax.ShapeDtypeStruct((b, s, N_HEADS), F32),
           jax.ShapeDtypeStruct((b, s, 2 * d), BF16)],
        compiler_params=pltpu.CompilerParams(
            dimension_semantics=("parallel", "parallel"), vmem_limit_bytes=V7X_VMEM_LIMIT),
        name="inproj",
    )(x, mod3, g_mix, w_k, w_qvt, w_f, w_g, b_f, b_g)


def _split3(v):
    hi = v.astype(BF16).astype(F32)
    r = v - hi
    mid = r.astype(BF16).astype(F32)
    lo = (r - mid).astype(BF16).astype(F32)
    return hi, mid, lo


def _aug_lane(h):
    return (h // 2) * PAIR + (h % 2) * AUG


def _cumsum_kernel(lf_ref, lft_ref, augk_ref, augqt_ref, *, chunk):
    s, h = lf_ref.shape
    n3 = 3 * h
    r = lax.broadcasted_iota(jnp.int32, (chunk, chunk), 0)
    c = lax.broadcasted_iota(jnp.int32, (chunk, chunk), 1)
    lower = (c <= r).astype(BF16)
    upper = (r <= c).astype(BF16)

    pr = lax.broadcasted_iota(jnp.int32, (n3, augk_ref.shape[1]), 0)
    pc = lax.broadcasted_iota(jnp.int32, (n3, augk_ref.shape[1]), 1)
    place_k = jnp.zeros((n3, augk_ref.shape[1]), F32)
    ones_k = jnp.zeros((1, augk_ref.shape[1]), F32)
    lane = lax.broadcasted_iota(jnp.int32, (1, augk_ref.shape[1]), 1)
    qr = lax.broadcasted_iota(jnp.int32, (h * AUG, n3), 0)
    qc = lax.broadcasted_iota(jnp.int32, (h * AUG, n3), 1)
    place_q = jnp.zeros((h * AUG, n3), F32)
    ones_q = jnp.zeros((h * AUG, 1), F32)
    qrow = lax.broadcasted_iota(jnp.int32, (h * AUG, 1), 0)
    for hh in range(h):
        for p in range(3):
            place_k = jnp.where((pr == p * h + hh) & (pc == _aug_lane(hh) + p), -1.0, place_k)
            ones_k = jnp.where(lane == _aug_lane(hh) + 3 + p, 1.0, ones_k)
            place_q = jnp.where((qr == hh * AUG + 3 + p) & (qc == p * h + hh), 1.0, place_q)
            ones_q = jnp.where(qrow == hh * AUG + p, 1.0, ones_q)
    place_k = place_k.astype(BF16)
    place_q = place_q.astype(BF16)

    carry_col = jnp.zeros((1, h), F32)
    carry_row = jnp.zeros((h, 1), F32)
    for ci in range(s // chunk):
        sl = slice(ci * chunk, (ci + 1) * chunk)
        pieces = jnp.concatenate(_split3(lf_ref[sl, :]), axis=1).astype(BF16)
        p3 = jnp.dot(lower, pieces, preferred_element_type=F32)
        cum = (p3[:, 0:h] + p3[:, h:2 * h]) + p3[:, 2 * h:n3] + carry_col
        carry_col = cum[chunk - 1:chunk, :]
        stacked = jnp.concatenate(_split3(cum), axis=1).astype(BF16)
        augk_ref[sl, :] = (jnp.dot(stacked, place_k, preferred_element_type=F32)
                           + ones_k).astype(BF16)
        pieces = jnp.concatenate(_split3(lft_ref[:, sl]), axis=0).astype(BF16)
        p3 = jnp.dot(pieces, upper, preferred_element_type=F32)
        cum = (p3[0:h] + p3[h:2 * h]) + p3[2 * h:n3] + carry_row
        carry_row = cum[:, chunk - 1:chunk]
        stacked = jnp.concatenate(_split3(cum), axis=0).astype(BF16)
        augqt_ref[:, sl] = (jnp.dot(place_q, stacked, preferred_element_type=F32)
                            + ones_q).astype(BF16)


def _cumsum_call(lf, lf_t):
    b, s, h = lf.shape
    return pl.pallas_call(
        functools.partial(_cumsum_kernel, chunk=min(256, s)),
        grid=(b,),
        in_specs=[pl.BlockSpec((None, s, h), lambda bi: (bi, 0, 0)),
                  pl.BlockSpec((None, h, s), lambda bi: (bi, 0, 0))],
        out_specs=[pl.BlockSpec((None, s, D_HEADS), lambda bi: (bi, 0, 0)),
                   pl.BlockSpec((None, h * AUG, s), lambda bi: (bi, 0, 0))],
        out_shape=[jax.ShapeDtypeStruct((b, s, D_HEADS), BF16),
                   jax.ShapeDtypeStruct((b, h * AUG, s), BF16)],
        compiler_params=pltpu.CompilerParams(dimension_semantics=("parallel",)),
        name="cumsum",
    )(lf, lf_t)


def _softplus2(z):
    neg_abs = lax.bitcast_convert_type(
        lax.bitcast_convert_type(z, jnp.uint32) | jnp.uint32(0x80000000), F32)
    return jnp.maximum(z, 0.0) + jnp.log2(1.0 + jnp.exp2(neg_abs))


def _fill_query_operand(rhs_ref, qt_ref, augqt_ref):
    rhs_ref[...] = jnp.zeros(rhs_ref.shape, rhs_ref.dtype)
    for hh in range(N_HEADS):
        half = hh % 2
        rhs_ref[hh, half * HEAD_DIM:(half + 1) * HEAD_DIM, :] = (
            qt_ref[hh * HEAD_DIM:(hh + 1) * HEAD_DIM, :])
        if augqt_ref is not None:
            rhs_ref[hh, PAIR + half * AUG:PAIR + (half + 1) * AUG, :] = (
                augqt_ref[hh * AUG:(hh + 1) * AUG, :])


def _sb_kernel(qt_ref, k_ref, vt_ref, o_ref, rhs_ref, z_ref, acc_ref, run_ref, *, tq):
    g = N_HEADS
    i = pl.program_id(1)
    _fill_query_operand(rhs_ref, qt_ref, None)
    rows = lax.broadcasted_iota(jnp.int32, (tq, tq), 0)
    cols = lax.broadcasted_iota(jnp.int32, (tq, tq), 1)
    suffix = (cols >= rows).astype(BF16)
    past = rows < cols

    def scores(j):
        start = pl.multiple_of(j * tq, tq)
        return [jnp.dot(k_ref[pl.ds(start, tq), (hh // 2) * PAIR:(hh // 2 + 1) * PAIR], rhs_ref[hh],
                        preferred_element_type=F32) for hh in range(g)]

    def block(j, masked):
        zs_next = scores(jnp.maximum(j - 1, 0))
        zs, cs, runs = [], [], []
        for hh in range(g):
            z = z_ref[hh]
            z_ref[hh] = zs_next[hh]
            sp = _softplus2(z)
            if masked:
                sp = jnp.where(past, sp, 0.0)
            cs.append(jnp.dot(suffix, sp.astype(BF16), preferred_element_type=F32))
            zs.append(z)
        for hh in range(g):
            run = run_ref[hh:hh + 1, :]
            runs.append(run)
            run_ref[hh:hh + 1, :] = run + cs[hh][0:1, :]
        lowest = jnp.min(run_ref[...])
        for hh in range(g):
            w = jnp.exp2(zs[hh] - cs[hh] - runs[hh])
            if masked:
                w = jnp.where(past, w, 0.0)
            vt = vt_ref[j, hh * HEAD_DIM:(hh + 1) * HEAD_DIM, :]
            acc_ref[hh] += jnp.dot(vt, w.astype(BF16), preferred_element_type=F32)
        return lowest

    for hh, z in enumerate(scores(i)):
        z_ref[hh] = z
    acc_ref[...] = jnp.zeros(acc_ref.shape, F32)
    run_ref[...] = jnp.zeros(run_ref.shape, F32)
    lowest = block(i, True)

    def cond(state):
        n, lowest = state
        return jnp.logical_and(n < i, lowest < SB_DEAD_LOG2)

    def body(state):
        n, _ = state
        return n + 1, block(i - 1 - n, False)

    lax.while_loop(cond, body, (jnp.int32(0), lowest))
    for hh in range(g):
        o_ref[hh * HEAD_DIM:(hh + 1) * HEAD_DIM, :] = acc_ref[hh].astype(o_ref.dtype)


def _fox_kernel(qt_ref, augqt_ref, k_ref, augk_ref, vt_ref, o_ref, rhs_ref, s_ref, acc_ref, m_ref,
                *, tq):
    g = N_HEADS
    i = pl.program_id(1)
    _fill_query_operand(rhs_ref, qt_ref, augqt_ref)
    rows = lax.broadcasted_iota(jnp.int32, (tq, tq), 0)
    cols = lax.broadcasted_iota(jnp.int32, (tq, tq), 1)
    causal = rows <= cols

    def scores(j):
        start = pl.multiple_of(j * tq, tq)
        out = []
        for hh in range(g):
            pair = slice((hh // 2) * PAIR, (hh // 2 + 1) * PAIR)
            lhs = jnp.concatenate([k_ref[pl.ds(start, tq), pair], augk_ref[pl.ds(start, tq), pair]],
                                  axis=1)
            out.append(jnp.dot(lhs, rhs_ref[hh], preferred_element_type=F32))
        return out

    def block(j, j_next, masked):
        ss_next = scores(j_next)
        alphas, ps = [], []
        for hh in range(g):
            s = s_ref[hh]
            s_ref[hh] = ss_next[hh]
            if masked:
                s = jnp.where(causal, s, NEG_BIG)
            m = m_ref[hh:hh + 1, :]
            m_new = jnp.maximum(m, jnp.max(s, axis=0, keepdims=True))
            alpha = jnp.exp2(m - m_new)
            p = jnp.exp2(s - m_new)
            m_ref[hh:hh + 1, :] = m_new
            alphas.append(alpha)
            ps.append(p.astype(BF16))
        for hh in range(g):
            vt = jnp.concatenate([vt_ref[j, hh * HEAD_DIM:(hh + 1) * HEAD_DIM, :], den_rows], axis=0)
            acc_ref[hh] = alphas[hh] * acc_ref[hh] + jnp.dot(vt, ps[hh],
                                                             preferred_element_type=F32)

    den_rows = jnp.ones((DEN_ROWS, tq), BF16)
    for hh, s in enumerate(scores(i)):
        s_ref[hh] = s
    acc_ref[...] = jnp.zeros(acc_ref.shape, F32)
    m_ref[...] = jnp.full(m_ref.shape, NEG_BIG, F32)
    block(i, 0, True)
    last = jnp.maximum(i - 1, 0)

    @pl.loop(0, i)
    def _(j):
        block(j, jnp.minimum(j + 1, last), False)

    for hh in range(g):
        o_ref[hh * HEAD_DIM:(hh + 1) * HEAD_DIM, :] = (
            acc_ref[hh, :HEAD_DIM, :] / acc_ref[hh, HEAD_DIM:HEAD_DIM + 1, :]).astype(o_ref.dtype)


def _attn_call(qt, k, vt, tq, aug=None):
    b, d, s = qt.shape
    nk = s // tq
    q_spec = pl.BlockSpec((None, d, tq), lambda bi, i: (bi, 0, i))
    k_spec = pl.BlockSpec((None, s, d), lambda bi, i: (bi, 0, 0))
    vt_spec = pl.BlockSpec((None, nk, d, tq), lambda bi, i: (bi, 0, 0, 0))
    if aug is None:
        body, name, kdim, acc_rows = _sb_kernel, "sb_attn", PAIR, HEAD_DIM
        operands, in_specs = (qt, k, vt), [q_spec, k_spec, vt_spec]
    else:
        augk, augqt = aug
        body, name, kdim, acc_rows = _fox_kernel, "fox_attn", 2 * PAIR, HEAD_DIM + DEN_ROWS
        operands = (qt, augqt, k, augk, vt)
        in_specs = [q_spec, pl.BlockSpec((None, N_HEADS * AUG, tq), lambda bi, i: (bi, 0, i)),
                    k_spec, k_spec, vt_spec]
    return pl.pallas_call(
        functools.partial(body, tq=tq),
        grid=(b, s // tq),
        in_specs=in_specs,
        out_specs=q_spec,
        out_shape=jax.ShapeDtypeStruct((b, d, s), BF16),
        scratch_shapes=[pltpu.VMEM((N_HEADS, kdim, tq), BF16),
                        pltpu.VMEM((N_HEADS, tq, tq), F32),
                        pltpu.VMEM((N_HEADS, acc_rows, tq), F32),
                        pltpu.VMEM((N_HEADS, tq), F32)],
        compiler_params=pltpu.CompilerParams(
            dimension_semantics=("parallel", "arbitrary"), vmem_limit_bytes=V7X_VMEM_LIMIT),
        name=name,
    )(*operands)


def _merge_kernel(x_ref, yst_ref, yft_ref, gate_ref, mod_ref, g_ref, ws_ref, wfx_ref, wo_ref,
                  x1_ref, h2_ref):
    d = x_ref.shape[-1]
    mod = mod_ref[...]
    a = jnp.dot(yst_ref[...].T, ws_ref[...], preferred_element_type=F32)
    b = jnp.dot(yft_ref[...].T, wfx_ref[...], preferred_element_type=F32)
    gate = gate_ref[...].astype(F32)
    merged = gate[:, :d] * a + gate[:, d:] * b
    x1 = x_ref[...] + mod[2:3, :] * jnp.dot(merged.astype(BF16), wo_ref[...],
                                            preferred_element_type=F32)
    x1_ref[...] = x1
    h2_ref[...] = _rms_modulate(x1, g_ref[...], mod[4:5, :], mod[3:4, :]).astype(BF16)


def _merge_call(x, yt_sb, yt_fx, gates, mod3, g_ffn, w_bs, w_bf, w_out, tm):
    b, s, d = x.shape
    row = lambda bi, i: (bi, i, 0)
    yt_spec = pl.BlockSpec((None, D_HEADS, tm), lambda bi, i: (bi, 0, i))
    return pl.pallas_call(
        _merge_kernel,
        grid=(b, s // tm),
        in_specs=[pl.BlockSpec((None, tm, d), row), yt_spec, yt_spec,
                  pl.BlockSpec((None, tm, 2 * d), row),
                  pl.BlockSpec((None, N_MOD, d), lambda bi, i: (bi, 0, 0)),
                  _const_spec((1, d)),
                  _const_spec(w_bs.shape), _const_spec(w_bf.shape), _const_spec(w_out.shape)],
        out_specs=[pl.BlockSpec((None, tm, d), row), pl.BlockSpec((None, tm, d), row)],
        out_shape=[jax.ShapeDtypeStruct((b, s, d), F32), jax.ShapeDtypeStruct((b, s, d), BF16)],
        compiler_params=pltpu.CompilerParams(
            dimension_semantics=("parallel", "parallel"), vmem_limit_bytes=V7X_VMEM_LIMIT),
        name="merge",
    )(x, yt_sb, yt_fx, gates, mod3, g_ffn, w_bs, w_bf, w_out)


def _ffn_kernel(x1_ref, h2_ref, mod_ref, g_ref, wg_ref, wu_ref, wd_ref, o_ref, *, n_chunks):
    h2 = h2_ref[...]
    dff = wg_ref.shape[1]
    cw = dff // n_chunks
    ffn = None
    for c in range(n_chunks):
        sl = slice(c * cw, (c + 1) * cw)
        gt = jnp.dot(h2, wg_ref[:, sl], preferred_element_type=F32)
        up = jnp.dot(h2, wu_ref[:, sl], preferred_element_type=F32)
        act = (gt * jax.nn.sigmoid(gt) * up).astype(BF16)
        part = jnp.dot(act, wd_ref[sl, :], preferred_element_type=F32)
        ffn = part if ffn is None else ffn + part
    x2 = x1_ref[...] + mod_ref[5:6, :] * ffn
    ms = jnp.mean(x2 * x2, axis=-1, keepdims=True)
    o_ref[...] = x2 * lax.rsqrt(ms + RMS_EPS) * g_ref[...]


def _ffn_call(x1, h2, mod3, g_final, w_gate, w_up, w_down, tm):
    b, s, d = x1.shape
    row = lambda bi, i: (bi, i, 0)
    dff = w_gate.shape[1]
    n_chunks = 2 if dff % 256 == 0 else 1
    return pl.pallas_call(
        functools.partial(_ffn_kernel, n_chunks=n_chunks),
        grid=(b, s // tm),
        in_specs=[pl.BlockSpec((None, tm, d), row),
                  pl.BlockSpec((None, tm, d), row),
                  pl.BlockSpec((None, N_MOD, d), lambda bi, i: (bi, 0, 0)),
                  _const_spec((1, d)),
                  _const_spec(w_gate.shape), _const_spec(w_up.shape), _const_spec(w_down.shape)],
        out_specs=pl.BlockSpec((None, tm, d), row),
        out_shape=jax.ShapeDtypeStruct((b, s, d), F32),
        compiler_params=pltpu.CompilerParams(
            dimension_semantics=("parallel", "parallel"), vmem_limit_bytes=V7X_VMEM_LIMIT),
        name="ffn",
    )(x1, h2, mod3, g_final, w_gate, w_up, w_down)


def kernel(x, c, w_ada, b_ada, g_mix, w_in, b_forget, b_gate, w_branch_sb, w_branch_fox,
           w_out, g_ffn, w_ffn_gate, w_ffn_up, w_ffn_down, g_final):
    b, s, d = x.shape
    depth = w_ada.shape[0]
    assert depth == 1, "the final RMSNorm is fused into the last (only) layer's FFN call"
    tm = min(512, s)
    tq = min(256, s)
    dh = D_HEADS
    for l in range(depth):
        mod3 = _mod_call(c, w_ada[l], b_ada[l]).reshape(b, N_MOD, d)
        w = w_in[l]
        w_k = jnp.concatenate([w[:, dh:2 * dh], w[:, 4 * dh:5 * dh]], axis=1).astype(BF16)
        w_qvt = jnp.concatenate([w[:, 0:dh], w[:, 3 * dh:4 * dh], w[:, 2 * dh:3 * dh],
                                 w[:, 5 * dh:6 * dh]], axis=1).T.astype(BF16)
        w_f = jnp.pad(w[:, 6 * dh:6 * dh + N_HEADS], ((0, 0), (0, 128 - N_HEADS))).astype(BF16)
        w_g = w[:, 6 * dh + N_HEADS:].astype(BF16)
        b_f = jnp.pad(b_forget[l], (0, 128 - N_HEADS)).reshape(1, 128)
        k_sb, k_fx, qt_sb, qt_fx, vt_sb, vt_fx, lf, gates = _inproj_call(
            x, mod3, g_mix[l].reshape(1, d), w_k, w_qvt, w_f, w_g, b_f,
            b_gate[l].reshape(1, 2 * d), tm, tq)

        yt_sb = _attn_call(qt_sb, k_sb, vt_sb, tq)
        aug = _cumsum_call(lf, lf.transpose(0, 2, 1))
        yt_fx = _attn_call(qt_fx, k_fx, vt_fx, tq, aug)

        x1, h2 = _merge_call(x, yt_sb, yt_fx, gates, mod3, g_ffn[l].reshape(1, d),
                             w_branch_sb[l].astype(BF16), w_branch_fox[l].astype(BF16),
                             w_out[l].astype(BF16), tm)
        x = _ffn_call(x1, h2, mod3, g_final.reshape(1, d), w_ffn_gate[l].astype(BF16),
                      w_ffn_up[l].astype(BF16), w_ffn_down[l].astype(BF16), tm)
    return x
```

```python
import functools
import math

import jax
import jax.numpy as jnp
from jax import lax
from jax.experimental import pallas as pl
from jax.experimental.pallas import tpu as pltpu

F32 = jnp.float32
BF16 = jnp.bfloat16

HEAD_DIM = 64
N_HEADS = 8
D_HEADS = N_HEADS * HEAD_DIM
PAIR = 2 * HEAD_DIM
N_MOD = 6
RMS_EPS = 1e-6
LOG2E = math.log2(math.e)
Q_SCALE = LOG2E * HEAD_DIM ** -0.5
AUG = 16
DEN_ROWS = 16
NEG_BIG = -1e30
SB_DEAD_LOG2 = 160.0
V7X_VMEM_LIMIT = 56 * 1024 * 1024
V7X_MXU_WIDTH = 256
NT_DIMS = (((1,), (1,)), ((), ()))


def _rms_modulate(x, g, scale, shift):
    ms = jnp.mean(x * x, axis=-1, keepdims=True)
    y = x * lax.rsqrt(ms + RMS_EPS) * g
    return y * (1.0 + scale) + shift


def _const_spec(shape):
    return pl.BlockSpec(shape, lambda *_: (0,) * len(shape), pipeline_mode=pl.Buffered(1))


def _mod_kernel(c_ref, w_ref, b_ref, o_ref):
    c = c_ref[...]
    ca = c * jax.nn.sigmoid(c)
    o_ref[...] = jnp.dot(ca, w_ref[...], preferred_element_type=F32,
                         precision=lax.Precision.HIGHEST) + b_ref[...]


def _mod_call(c, w_ada, b_ada):
    b, d = c.shape
    n = w_ada.shape[1]
    tn = 1024
    return pl.pallas_call(
        _mod_kernel,
        grid=(n // tn,),
        in_specs=[pl.BlockSpec((b, d), lambda j: (0, 0)),
                  pl.BlockSpec((d, tn), lambda j: (0, j)),
                  pl.BlockSpec((1, tn), lambda j: (0, j))],
        out_specs=pl.BlockSpec((b, tn), lambda j: (0, j)),
        out_shape=jax.ShapeDtypeStruct((b, n), F32),
        name="mod",
    )(c, w_ada, b_ada.reshape(1, n))


def _inproj_kernel(x_ref, mod_ref, g_ref, wk_ref, wqvt_ref, wf_ref, wg_ref, bf_ref, bg_ref,
                   ks_ref, kf_ref, qts_ref, qtf_ref, vts_ref, vtf_ref, lf_ref, gate_ref, *, tk):
    mod = mod_ref[...]
    h = _rms_modulate(x_ref[...], g_ref[...], mod[1:2, :], mod[0:1, :]).astype(BF16)
    d = D_HEADS
    kk = jnp.dot(h, wk_ref[...], preferred_element_type=F32)
    ks_ref[...] = kk[:, :d].astype(BF16)
    kf_ref[...] = kk[:, d:].astype(BF16)
    t = lax.dot_general(wqvt_ref[...], h, NT_DIMS, preferred_element_type=F32)
    qts_ref[...] = (t[0 * d:1 * d] * Q_SCALE).astype(BF16)
    qtf_ref[...] = (t[1 * d:2 * d] * Q_SCALE).astype(BF16)
    for c in range(vts_ref.shape[0]):
        vts_ref[c] = t[2 * d:3 * d, c * tk:(c + 1) * tk].astype(BF16)
        vtf_ref[c] = t[3 * d:4 * d, c * tk:(c + 1) * tk].astype(BF16)
    f = jnp.dot(h, wf_ref[...], preferred_element_type=F32) + bf_ref[...]
    lf = -(jnp.maximum(-f, 0.0) + jnp.log(1.0 + jnp.exp(-jnp.abs(f)))) * LOG2E
    lf_ref[...] = lf[:, :N_HEADS]
    gl = jnp.dot(h, wg_ref[...], preferred_element_type=F32) + bg_ref[...]
    gate_ref[...] = jax.nn.sigmoid(gl).astype(BF16)


def _inproj_call(x, mod3, g_mix, w_k, w_qvt, w_f, w_g, b_f, b_g, tm, tk):
    b, s, d = x.shape
    row = lambda bi, i: (bi, i, 0)
    col = lambda bi, i: (bi, 0, i)
    k_spec = pl.BlockSpec((None, tm, D_HEADS), row)
    qt_spec = pl.BlockSpec((None, D_HEADS, tm), col)
    vt_spec = pl.BlockSpec((None, tm // tk, D_HEADS, tk), lambda bi, i: (bi, i, 0, 0))
    return pl.pallas_call(
        functools.partial(_inproj_kernel, tk=tk),
        grid=(b, s // tm),
        in_specs=[pl.BlockSpec((None, tm, d), row),
                  pl.BlockSpec((None, N_MOD, d), lambda bi, i: (bi, 0, 0)),
                  _const_spec((1, d)),
                  _const_spec(w_k.shape), _const_spec(w_qvt.shape), _const_spec(w_f.shape),
                  _const_spec(w_g.shape), _const_spec(b_f.shape), _const_spec(b_g.shape)],
        out_specs=[k_spec, k_spec, qt_spec, qt_spec, vt_spec, vt_spec,
                   pl.BlockSpec((None, tm, N_HEADS), row),
                   pl.BlockSpec((None, tm, 2 * d), row)],
        out_shape=[jax.ShapeDtypeStruct((b, s, D_HEADS), BF16)] * 2
        + [jax.ShapeDtypeStruct((b, D_HEADS, s), BF16)] * 2
        + [jax.ShapeDtypeStruct((b, s // tk, D_HEADS, tk), BF16)] * 2
        + [jax.ShapeDtypeStruct((b, s, N_HEADS), F32),
           jax.ShapeDtypeStruct((b, s, 2 * d), BF16)],
        compiler_params=pltpu.CompilerParams(
            dimension_semantics=("parallel", "parallel"), vmem_limit_bytes=V7X_VMEM_LIMIT),
        name="inproj",
    )(x, mod3, g_mix, w_k, w_qvt, w_f, w_g, b_f, b_g)


def _split3(v):
    hi = v.astype(BF16).astype(F32)
    r = v - hi
    mid = r.astype(BF16).astype(F32)
    lo = (r - mid).astype(BF16).astype(F32)
    return hi, mid, lo


def _aug_lane(h):
    return (h // 2) * PAIR + (h % 2) * AUG


def _cumsum_kernel(lf_ref, lft_ref, augk_ref, augqt_ref, *, chunk):
    s, h = lf_ref.shape
    n3 = 3 * h
    r = lax.broadcasted_iota(jnp.int32, (chunk, chunk), 0)
    c = lax.broadcasted_iota(jnp.int32, (chunk, chunk), 1)
    lower = (c <= r).astype(BF16)
    upper = (r <= c).astype(BF16)

    pr = lax.broadcasted_iota(jnp.int32, (n3, augk_ref.shape[1]), 0)
    pc = lax.broadcasted_iota(jnp.int32, (n3, augk_ref.shape[1]), 1)
    place_k = jnp.zeros((n3, augk_ref.shape[1]), F32)
    ones_k = jnp.zeros((1, augk_ref.shape[1]), F32)
    lane = lax.broadcasted_iota(jnp.int32, (1, augk_ref.shape[1]), 1)
    qr = lax.broadcasted_iota(jnp.int32, (h * AUG, n3), 0)
    qc = lax.broadcasted_iota(jnp.int32, (h * AUG, n3), 1)
    place_q = jnp.zeros((h * AUG, n3), F32)
    ones_q = jnp.zeros((h * AUG, 1), F32)
    qrow = lax.broadcasted_iota(jnp.int32, (h * AUG, 1), 0)
    for hh in range(h):
        for p in range(3):
            place_k = jnp.where((pr == p * h + hh) & (pc == _aug_lane(hh) + p), -1.0, place_k)
            ones_k = jnp.where(lane == _aug_lane(hh) + 3 + p, 1.0, ones_k)
            place_q = jnp.where((qr == hh * AUG + 3 + p) & (qc == p * h + hh), 1.0, place_q)
            ones_q = jnp.where(qrow == hh * AUG + p, 1.0, ones_q)
    place_k = place_k.astype(BF16)
    place_q = place_q.astype(BF16)

    carry_col = jnp.zeros((1, h), F32)
    carry_row = jnp.zeros((h, 1), F32)
    for ci in range(s // chunk):
        sl = slice(ci * chunk, (ci + 1) * chunk)
        pieces = jnp.concatenate(_split3(lf_ref[sl, :]), axis=1).astype(BF16)
        p3 = jnp.dot(lower, pieces, preferred_element_type=F32)
        cum = (p3[:, 0:h] + p3[:, h:2 * h]) + p3[:, 2 * h:n3] + carry_col
        carry_col = cum[chunk - 1:chunk, :]
        stacked = jnp.concatenate(_split3(cum), axis=1).astype(BF16)
        augk_ref[sl, :] = (jnp.dot(stacked, place_k, preferred_element_type=F32)
                           + ones_k).astype(BF16)
        pieces = jnp.concatenate(_split3(lft_ref[:, sl]), axis=0).astype(BF16)
        p3 = jnp.dot(pieces, upper, preferred_element_type=F32)
        cum = (p3[0:h] + p3[h:2 * h]) + p3[2 * h:n3] + carry_row
        carry_row = cum[:, chunk - 1:chunk]
        stacked = jnp.concatenate(_split3(cum), axis=0).astype(BF16)
        augqt_ref[:, sl] = (jnp.dot(place_q, stacked, preferred_element_type=F32)
                            + ones_q).astype(BF16)


def _cumsum_call(lf, lf_t):
    b, s, h = lf.shape
    return pl.pallas_call(
        functools.partial(_cumsum_kernel, chunk=min(256, s)),
        grid=(b,),
        in_specs=[pl.BlockSpec((None, s, h), lambda bi: (bi, 0, 0)),
                  pl.BlockSpec((None, h, s), lambda bi: (bi, 0, 0))],
        out_specs=[pl.BlockSpec((None, s, D_HEADS), lambda bi: (bi, 0, 0)),
                   pl.BlockSpec((None, h * AUG, s), lambda bi: (bi, 0, 0))],
        out_shape=[jax.ShapeDtypeStruct((b, s, D_HEADS), BF16),
                   jax.ShapeDtypeStruct((b, h * AUG, s), BF16)],
        compiler_params=pltpu.CompilerParams(dimension_semantics=("parallel",)),
        name="cumsum",
    )(lf, lf_t)


def _softplus2(z):
    return jnp.maximum(z, 0.0) + jnp.log2(1.0 + jnp.exp2(-jnp.abs(z)))


def _fill_query_operand(rhs_ref, qt_ref, augqt_ref):
    rhs_ref[...] = jnp.zeros(rhs_ref.shape, rhs_ref.dtype)
    for hh in range(N_HEADS):
        half = hh % 2
        rhs_ref[hh, half * HEAD_DIM:(half + 1) * HEAD_DIM, :] = (
            qt_ref[hh * HEAD_DIM:(hh + 1) * HEAD_DIM, :])
        if augqt_ref is not None:
            rhs_ref[hh, PAIR + half * AUG:PAIR + (half + 1) * AUG, :] = (
                augqt_ref[hh * AUG:(hh + 1) * AUG, :])


def _sb_kernel(qt_ref, k_ref, vt_ref, o_ref, rhs_ref, z_ref, acc_ref, run_ref, *, tq):
    g = N_HEADS
    i = pl.program_id(1)
    _fill_query_operand(rhs_ref, qt_ref, None)
    rows = lax.broadcasted_iota(jnp.int32, (tq, tq), 0)
    cols = lax.broadcasted_iota(jnp.int32, (tq, tq), 1)
    suffix = (cols >= rows).astype(BF16)
    past = rows < cols

    def scores(j):
        start = pl.multiple_of(j * tq, tq)
        return [jnp.dot(k_ref[pl.ds(start, tq), (hh // 2) * PAIR:(hh // 2 + 1) * PAIR], rhs_ref[hh],
                        preferred_element_type=F32) for hh in range(g)]

    def block(j, masked):
        zs_next = scores(jnp.maximum(j - 1, 0))
        zs, cs, runs = [], [], []
        for hh in range(g):
            z = z_ref[hh]
            z_ref[hh] = zs_next[hh]
            sp = _softplus2(z)
            if masked:
                sp = jnp.where(past, sp, 0.0)
            cs.append(jnp.dot(suffix, sp.astype(BF16), preferred_element_type=F32))
            zs.append(z)
        for hh in range(g):
            run = run_ref[hh:hh + 1, :]
            runs.append(run)
            run_ref[hh:hh + 1, :] = run + cs[hh][0:1, :]
        lowest = jnp.min(run_ref[...])
        for hh in range(g):
            w = jnp.exp2(zs[hh] - cs[hh] - runs[hh])
            if masked:
                w = jnp.where(past, w, 0.0)
            vt = vt_ref[j, hh * HEAD_DIM:(hh + 1) * HEAD_DIM, :]
            acc_ref[hh] += jnp.dot(vt, w.astype(BF16), preferred_element_type=F32)
        return lowest

    for hh, z in enumerate(scores(i)):
        z_ref[hh] = z
    acc_ref[...] = jnp.zeros(acc_ref.shape, F32)
    run_ref[...] = jnp.zeros(run_ref.shape, F32)
    lowest = block(i, True)

    def cond(state):
        n, lowest = state
        return jnp.logical_and(n < i, lowest < SB_DEAD_LOG2)

    def body(state):
        n, _ = state
        return n + 1, block(i - 1 - n, False)

    lax.while_loop(cond, body, (jnp.int32(0), lowest))
    for hh in range(g):
        o_ref[hh * HEAD_DIM:(hh + 1) * HEAD_DIM, :] = acc_ref[hh].astype(o_ref.dtype)


def _fox_kernel(qt_ref, augqt_ref, k_ref, augk_ref, vt_ref, o_ref, rhs_ref, s_ref, acc_ref, m_ref,
                *, tq):
    g = N_HEADS
    i = pl.program_id(1)
    _fill_query_operand(rhs_ref, qt_ref, augqt_ref)
    rows = lax.broadcasted_iota(jnp.int32, (tq, tq), 0)
    cols = lax.broadcasted_iota(jnp.int32, (tq, tq), 1)
    causal = rows <= cols

    def scores(j):
        start = pl.multiple_of(j * tq, tq)
        out = []
        for hh in range(g):
            pair = slice((hh // 2) * PAIR, (hh // 2 + 1) * PAIR)
            lhs = jnp.concatenate([k_ref[pl.ds(start, tq), pair], augk_ref[pl.ds(start, tq), pair]],
                                  axis=1)
            out.append(jnp.dot(lhs, rhs_ref[hh], preferred_element_type=F32))
        return out

    def block(j, j_next, masked):
        ss_next = scores(j_next)
        alphas, ps = [], []
        for hh in range(g):
            s = s_ref[hh]
            s_ref[hh] = ss_next[hh]
            if masked:
                s = jnp.where(causal, s, NEG_BIG)
            m = m_ref[hh:hh + 1, :]
            m_new = jnp.maximum(m, jnp.max(s, axis=0, keepdims=True))
            alpha = jnp.exp2(m - m_new)
            p = jnp.exp2(s - m_new)
            m_ref[hh:hh + 1, :] = m_new
            alphas.append(alpha)
            ps.append(p.astype(BF16))
        for hh in range(g):
            vt = jnp.concatenate([vt_ref[j, hh * HEAD_DIM:(hh + 1) * HEAD_DIM, :], den_rows], axis=0)
            acc_ref[hh] = alphas[hh] * acc_ref[hh] + jnp.dot(vt, ps[hh],
                                                             preferred_element_type=F32)

    den_rows = jnp.ones((DEN_ROWS, tq), BF16)
    for hh, s in enumerate(scores(i)):
        s_ref[hh] = s
    acc_ref[...] = jnp.zeros(acc_ref.shape, F32)
    m_ref[...] = jnp.full(m_ref.shape, NEG_BIG, F32)
    block(i, 0, True)
    last = jnp.maximum(i - 1, 0)

    @pl.loop(0, i // 2)
    def _(n):
        block(2 * n, 2 * n + 1, False)
        block(2 * n + 1, jnp.minimum(2 * n + 2, last), False)

    @pl.when(i % 2 == 1)
    def _():
        block(last, last, False)

    for hh in range(g):
        o_ref[hh * HEAD_DIM:(hh + 1) * HEAD_DIM, :] = (
            acc_ref[hh, :HEAD_DIM, :] / acc_ref[hh, HEAD_DIM:HEAD_DIM + 1, :]).astype(o_ref.dtype)


def _attn_call(qt, k, vt, tq, aug=None):
    b, d, s = qt.shape
    nk = s // tq
    q_spec = pl.BlockSpec((None, d, tq), lambda bi, i: (bi, 0, i))
    k_spec = pl.BlockSpec((None, s, d), lambda bi, i: (bi, 0, 0))
    vt_spec = pl.BlockSpec((None, nk, d, tq), lambda bi, i: (bi, 0, 0, 0))
    if aug is None:
        body, name, kdim, acc_rows = _sb_kernel, "sb_attn", PAIR, HEAD_DIM
        operands, in_specs = (qt, k, vt), [q_spec, k_spec, vt_spec]
    else:
        augk, augqt = aug
        body, name, kdim, acc_rows = _fox_kernel, "fox_attn", 2 * PAIR, HEAD_DIM + DEN_ROWS
        operands = (qt, augqt, k, augk, vt)
        in_specs = [q_spec, pl.BlockSpec((None, N_HEADS * AUG, tq), lambda bi, i: (bi, 0, i)),
                    k_spec, k_spec, vt_spec]
    return pl.pallas_call(
        functools.partial(body, tq=tq),
        grid=(b, s // tq),
        in_specs=in_specs,
        out_specs=q_spec,
        out_shape=jax.ShapeDtypeStruct((b, d, s), BF16),
        scratch_shapes=[pltpu.VMEM((N_HEADS, kdim, tq), BF16),
                        pltpu.VMEM((N_HEADS, tq, tq), F32),
                        pltpu.VMEM((N_HEADS, acc_rows, tq), F32),
                        pltpu.VMEM((N_HEADS, tq), F32)],
        compiler_params=pltpu.CompilerParams(
            dimension_semantics=("parallel", "arbitrary"), vmem_limit_bytes=V7X_VMEM_LIMIT),
        name=name,
    )(*operands)


def _merge_kernel(x_ref, yst_ref, yft_ref, gate_ref, mod_ref, g_ref, ws_ref, wfx_ref, wo_ref,
                  x1_ref, h2_ref):
    d = x_ref.shape[-1]
    mod = mod_ref[...]
    a = jnp.dot(yst_ref[...].T, ws_ref[...], preferred_element_type=F32)
    b = jnp.dot(yft_ref[...].T, wfx_ref[...], preferred_element_type=F32)
    gate = gate_ref[...].astype(F32)
    merged = gate[:, :d] * a + gate[:, d:] * b
    x1 = x_ref[...] + mod[2:3, :] * jnp.dot(merged.astype(BF16), wo_ref[...],
                                            preferred_element_type=F32)
    x1_ref[...] = x1
    h2_ref[...] = _rms_modulate(x1, g_ref[...], mod[4:5, :], mod[3:4, :]).astype(BF16)


def _merge_call(x, yt_sb, yt_fx, gates, mod3, g_ffn, w_bs, w_bf, w_out, tm):
    b, s, d = x.shape
    row = lambda bi, i: (bi, i, 0)
    yt_spec = pl.BlockSpec((None, D_HEADS, tm), lambda bi, i: (bi, 0, i))
    return pl.pallas_call(
        _merge_kernel,
        grid=(b, s // tm),
        in_specs=[pl.BlockSpec((None, tm, d), row), yt_spec, yt_spec,
                  pl.BlockSpec((None, tm, 2 * d), row),
                  pl.BlockSpec((None, N_MOD, d), lambda bi, i: (bi, 0, 0)),
                  _const_spec((1, d)),
                  _const_spec(w_bs.shape), _const_spec(w_bf.shape), _const_spec(w_out.shape)],
        out_specs=[pl.BlockSpec((None, tm, d), row), pl.BlockSpec((None, tm, d), row)],
        out_shape=[jax.ShapeDtypeStruct((b, s, d), F32), jax.ShapeDtypeStruct((b, s, d), BF16)],
        compiler_params=pltpu.CompilerParams(
            dimension_semantics=("parallel", "parallel"), vmem_limit_bytes=V7X_VMEM_LIMIT),
        name="merge",
    )(x, yt_sb, yt_fx, gates, mod3, g_ffn, w_bs, w_bf, w_out)


def _ffn_kernel(x1_ref, h2_ref, mod_ref, g_ref, wg_ref, wu_ref, wd_ref, o_ref, *, bounds):
    h2 = h2_ref[...]
    ffn = None
    for lo, hi in zip(bounds[:-1], bounds[1:]):
        sl = slice(lo, hi)
        gt = jnp.dot(h2, wg_ref[:, sl], preferred_element_type=F32)
        up = jnp.dot(h2, wu_ref[:, sl], preferred_element_type=F32)
        act = (gt * jax.nn.sigmoid(gt) * up).astype(BF16)
        part = jnp.dot(act, wd_ref[sl, :], preferred_element_type=F32)
        ffn = part if ffn is None else ffn + part
    x2 = x1_ref[...] + mod_ref[5:6, :] * ffn
    ms = jnp.mean(x2 * x2, axis=-1, keepdims=True)
    o_ref[...] = x2 * lax.rsqrt(ms + RMS_EPS) * g_ref[...]


def _ffn_call(x1, h2, mod3, g_final, w_gate, w_up, w_down, tm):
    b, s, d = x1.shape
    row = lambda bi, i: (bi, i, 0)
    dff = w_gate.shape[1]
    half = -(-dff // (2 * V7X_MXU_WIDTH)) * V7X_MXU_WIDTH
    bounds = (0, half, dff) if half < dff else (0, dff)
    return pl.pallas_call(
        functools.partial(_ffn_kernel, bounds=bounds),
        grid=(b, s // tm),
        in_specs=[pl.BlockSpec((None, tm, d), row),
                  pl.BlockSpec((None, tm, d), row),
                  pl.BlockSpec((None, N_MOD, d), lambda bi, i: (bi, 0, 0)),
                  _const_spec((1, d)),
                  _const_spec(w_gate.shape), _const_spec(w_up.shape), _const_spec(w_down.shape)],
        out_specs=pl.BlockSpec((None, tm, d), row),
        out_shape=jax.ShapeDtypeStruct((b, s, d), F32),
        compiler_params=pltpu.CompilerParams(
            dimension_semantics=("parallel", "parallel"), vmem_limit_bytes=V7X_VMEM_LIMIT),
        name="ffn",
    )(x1, h2, mod3, g_final, w_gate, w_up, w_down)


def kernel(x, c, w_ada, b_ada, g_mix, w_in, b_forget, b_gate, w_branch_sb, w_branch_fox,
           w_out, g_ffn, w_ffn_gate, w_ffn_up, w_ffn_down, g_final):
    b, s, d = x.shape
    depth = w_ada.shape[0]
    assert depth == 1, "the final RMSNorm is fused into the last (only) layer's FFN call"
    tm = min(512, s)
    tq = min(256, s)
    dh = D_HEADS
    for l in range(depth):
        mod3 = _mod_call(c, w_ada[l], b_ada[l]).reshape(b, N_MOD, d)
        w = w_in[l]
        w_k = jnp.concatenate([w[:, dh:2 * dh], w[:, 4 * dh:5 * dh]], axis=1).astype(BF16)
        w_qvt = jnp.concatenate([w[:, 0:dh], w[:, 3 * dh:4 * dh], w[:, 2 * dh:3 * dh],
                                 w[:, 5 * dh:6 * dh]], axis=1).T.astype(BF16)
        w_f = jnp.pad(w[:, 6 * dh:6 * dh + N_HEADS], ((0, 0), (0, 128 - N_HEADS))).astype(BF16)
        w_g = w[:, 6 * dh + N_HEADS:].astype(BF16)
        b_f = jnp.pad(b_forget[l], (0, 128 - N_HEADS)).reshape(1, 128)
        k_sb, k_fx, qt_sb, qt_fx, vt_sb, vt_fx, lf, gates = _inproj_call(
            x, mod3, g_mix[l].reshape(1, d), w_k, w_qvt, w_f, w_g, b_f,
            b_gate[l].reshape(1, 2 * d), tm, tq)

        yt_sb = _attn_call(qt_sb, k_sb, vt_sb, tq)
        aug = _cumsum_call(lf, lf.transpose(0, 2, 1))
        yt_fx = _attn_call(qt_fx, k_fx, vt_fx, tq, aug)

        x1, h2 = _merge_call(x, yt_sb, yt_fx, gates, mod3, g_ffn[l].reshape(1, d),
                             w_branch_sb[l].astype(BF16), w_branch_fox[l].astype(BF16),
                             w_out[l].astype(BF16), tm)
        x = _ffn_call(x1, h2, mod3, g_final.reshape(1, d), w_ffn_gate[l].astype(BF16),
                      w_ffn_up[l].astype(BF16), w_ffn_down[l].astype(BF16), tm)
    return x
```

```python
import functools
import math

import jax
import jax.numpy as jnp
from jax import lax
from jax.experimental import pallas as pl
from jax.experimental.pallas import tpu as pltpu

F32 = jnp.float32
BF16 = jnp.bfloat16

HEAD_DIM = 64
N_HEADS = 8
D_HEADS = N_HEADS * HEAD_DIM
PAIR = 2 * HEAD_DIM
N_MOD = 6
RMS_EPS = 1e-6
LOG2E = math.log2(math.e)
Q_SCALE = LOG2E * HEAD_DIM ** -0.5
AUG = 16
DEN_ROWS = 16
F_PAD_ROWS = 16
NEG_BIG = -1e30
SB_DEAD_LOG2 = 160.0
V7X_VMEM_LIMIT = 56 * 1024 * 1024
V7X_MXU_WIDTH = 256
NT_DIMS = (((1,), (1,)), ((), ()))


def _rms_modulate(x, g, scale, shift):
    ms = jnp.mean(x * x, axis=-1, keepdims=True)
    y = x * lax.rsqrt(ms + RMS_EPS) * g
    return y * (1.0 + scale) + shift


def _const_spec(shape):
    return pl.BlockSpec(shape, lambda *_: (0,) * len(shape), pipeline_mode=pl.Buffered(1))


def _mod_kernel(c_ref, w_ref, b_ref, o_ref):
    c = c_ref[...]
    ca = c * jax.nn.sigmoid(c)
    o_ref[...] = jnp.dot(ca, w_ref[...], preferred_element_type=F32,
                         precision=lax.Precision.HIGHEST) + b_ref[...]


def _mod_call(c, w_ada, b_ada):
    b, d = c.shape
    n = w_ada.shape[1]
    tn = 1024
    return pl.pallas_call(
        _mod_kernel,
        grid=(n // tn,),
        in_specs=[pl.BlockSpec((b, d), lambda j: (0, 0)),
                  pl.BlockSpec((d, tn), lambda j: (0, j)),
                  pl.BlockSpec((1, tn), lambda j: (0, j))],
        out_specs=pl.BlockSpec((b, tn), lambda j: (0, j)),
        out_shape=jax.ShapeDtypeStruct((b, n), F32),
        name="mod",
    )(c, w_ada, b_ada.reshape(1, n))


def _inproj_kernel(x_ref, mod_ref, g_ref, wk_ref, wqvt_ref, wg_ref, bf_ref, bg_ref,
                   ks_ref, kf_ref, qts_ref, qtf_ref, vts_ref, vtf_ref, lft_ref, gate_ref, *, tk):
    mod = mod_ref[...]
    h = _rms_modulate(x_ref[...], g_ref[...], mod[1:2, :], mod[0:1, :]).astype(BF16)
    d = D_HEADS
    kk = jnp.dot(h, wk_ref[...], preferred_element_type=F32)
    ks_ref[...] = kk[:, :d].astype(BF16)
    kf_ref[...] = kk[:, d:].astype(BF16)
    t = lax.dot_general(wqvt_ref[...], h, NT_DIMS, preferred_element_type=F32)
    qts_ref[...] = (t[0 * d:1 * d] * Q_SCALE).astype(BF16)
    qtf_ref[...] = (t[1 * d:2 * d] * Q_SCALE).astype(BF16)
    for c in range(vts_ref.shape[0]):
        vts_ref[c] = t[2 * d:3 * d, c * tk:(c + 1) * tk].astype(BF16)
        vtf_ref[c] = t[3 * d:4 * d, c * tk:(c + 1) * tk].astype(BF16)
    f = t[4 * d:4 * d + N_HEADS] + bf_ref[...]
    lft_ref[...] = -(jnp.maximum(-f, 0.0) + jnp.log(1.0 + jnp.exp(-jnp.abs(f)))) * LOG2E
    gl = jnp.dot(h, wg_ref[...], preferred_element_type=F32) + bg_ref[...]
    gate_ref[...] = jax.nn.sigmoid(gl).astype(BF16)


def _inproj_call(x, mod3, g_mix, w_k, w_qvt, w_g, b_f, b_g, tm, tk):
    b, s, d = x.shape
    row = lambda bi, i: (bi, i, 0)
    col = lambda bi, i: (bi, 0, i)
    k_spec = pl.BlockSpec((None, tm, D_HEADS), row)
    qt_spec = pl.BlockSpec((None, D_HEADS, tm), col)
    vt_spec = pl.BlockSpec((None, tm // tk, D_HEADS, tk), lambda bi, i: (bi, i, 0, 0))
    return pl.pallas_call(
        functools.partial(_inproj_kernel, tk=tk),
        grid=(b, s // tm),
        in_specs=[pl.BlockSpec((None, tm, d), row),
                  pl.BlockSpec((None, N_MOD, d), lambda bi, i: (bi, 0, 0)),
                  _const_spec((1, d)),
                  _const_spec(w_k.shape), _const_spec(w_qvt.shape),
                  _const_spec(w_g.shape), _const_spec(b_f.shape), _const_spec(b_g.shape)],
        out_specs=[k_spec, k_spec, qt_spec, qt_spec, vt_spec, vt_spec,
                   pl.BlockSpec((None, N_HEADS, tm), col),
                   pl.BlockSpec((None, tm, 2 * d), row)],
        out_shape=[jax.ShapeDtypeStruct((b, s, D_HEADS), BF16)] * 2
        + [jax.ShapeDtypeStruct((b, D_HEADS, s), BF16)] * 2
        + [jax.ShapeDtypeStruct((b, s // tk, D_HEADS, tk), BF16)] * 2
        + [jax.ShapeDtypeStruct((b, N_HEADS, s), F32),
           jax.ShapeDtypeStruct((b, s, 2 * d), BF16)],
        compiler_params=pltpu.CompilerParams(
            dimension_semantics=("parallel", "parallel"), vmem_limit_bytes=V7X_VMEM_LIMIT),
        name="inproj",
    )(x, mod3, g_mix, w_k, w_qvt, w_g, b_f, b_g)


def _split3(v):
    hi = v.astype(BF16).astype(F32)
    r = v - hi
    mid = r.astype(BF16).astype(F32)
    lo = (r - mid).astype(BF16).astype(F32)
    return hi, mid, lo


def _cumsum_kernel(lft_ref, augk_ref, augqt_ref, *, chunk):
    h, s = lft_ref.shape
    n3 = 3 * h
    r = lax.broadcasted_iota(jnp.int32, (chunk, chunk), 0)
    c = lax.broadcasted_iota(jnp.int32, (chunk, chunk), 1)
    upper = (r <= c).astype(BF16)

    qr = lax.broadcasted_iota(jnp.int32, (h * AUG, n3), 0)
    qc = lax.broadcasted_iota(jnp.int32, (h * AUG, n3), 1)
    qrow = lax.broadcasted_iota(jnp.int32, (h * AUG, 1), 0)
    place_q = jnp.zeros((h * AUG, n3), F32)
    place_k = jnp.zeros((h * AUG, n3), F32)
    ones_q = jnp.zeros((h * AUG, 1), F32)
    ones_k = jnp.zeros((h * AUG, 1), F32)
    for hh in range(h):
        for p in range(3):
            place_q = jnp.where((qr == hh * AUG + 3 + p) & (qc == p * h + hh), 1.0, place_q)
            place_k = jnp.where((qr == hh * AUG + p) & (qc == p * h + hh), -1.0, place_k)
            ones_q = jnp.where(qrow == hh * AUG + p, 1.0, ones_q)
            ones_k = jnp.where(qrow == hh * AUG + 3 + p, 1.0, ones_k)
    place_q = place_q.astype(BF16)
    place_k = place_k.astype(BF16)

    carry = jnp.zeros((h, 1), F32)
    for ci in range(s // chunk):
        sl = slice(ci * chunk, (ci + 1) * chunk)
        pieces = jnp.concatenate(_split3(lft_ref[:, sl]), axis=0).astype(BF16)
        p3 = jnp.dot(pieces, upper, preferred_element_type=F32)
        cum = (p3[0:h] + p3[h:2 * h]) + p3[2 * h:n3] + carry
        carry = cum[:, chunk - 1:chunk]
        stacked = jnp.concatenate(_split3(cum), axis=0).astype(BF16)
        augqt_ref[:, sl] = (jnp.dot(place_q, stacked, preferred_element_type=F32)
                            + ones_q).astype(BF16)
        augkt = jnp.dot(place_k, stacked, preferred_element_type=F32) + ones_k
        augk_ref[sl, :] = augkt.T.astype(BF16)


def _cumsum_call(lf_t):
    b, h, s = lf_t.shape
    return pl.pallas_call(
        functools.partial(_cumsum_kernel, chunk=min(256, s)),
        grid=(b,),
        in_specs=[pl.BlockSpec((None, h, s), lambda bi: (bi, 0, 0))],
        out_specs=[pl.BlockSpec((None, s, h * AUG), lambda bi: (bi, 0, 0)),
                   pl.BlockSpec((None, h * AUG, s), lambda bi: (bi, 0, 0))],
        out_shape=[jax.ShapeDtypeStruct((b, s, h * AUG), BF16),
                   jax.ShapeDtypeStruct((b, h * AUG, s), BF16)],
        compiler_params=pltpu.CompilerParams(dimension_semantics=("parallel",)),
        name="cumsum",
    )(lf_t)


def _softplus2(z):
    return jnp.maximum(z, 0.0) + jnp.log(1.0 + jnp.exp2(-jnp.abs(z))) * LOG2E


def _fill_query_operand(rhs_ref, qt_ref, augqt_ref):
    rhs_ref[...] = jnp.zeros(rhs_ref.shape, rhs_ref.dtype)
    for hh in range(N_HEADS):
        half = hh % 2
        rhs_ref[hh, half * HEAD_DIM:(half + 1) * HEAD_DIM, :] = (
            qt_ref[hh * HEAD_DIM:(hh + 1) * HEAD_DIM, :])
        if augqt_ref is not None:
            rhs_ref[hh, PAIR + hh * AUG:PAIR + (hh + 1) * AUG, :] = (
                augqt_ref[hh * AUG:(hh + 1) * AUG, :])


def _sb_kernel(qt_ref, k_ref, vt_ref, o_ref, rhs_ref, z_ref, acc_ref, run_ref, *, tq):
    g = N_HEADS
    i = pl.program_id(1)
    _fill_query_operand(rhs_ref, qt_ref, None)
    rows = lax.broadcasted_iota(jnp.int32, (tq, tq), 0)
    cols = lax.broadcasted_iota(jnp.int32, (tq, tq), 1)
    suffix = (cols >= rows).astype(BF16)
    past = rows < cols

    def scores(j):
        start = pl.multiple_of(j * tq, tq)
        return [jnp.dot(k_ref[pl.ds(start, tq), (hh // 2) * PAIR:(hh // 2 + 1) * PAIR], rhs_ref[hh],
                        preferred_element_type=F32) for hh in range(g)]

    def block(j, masked):
        zs_next = scores(jnp.maximum(j - 1, 0))
        zs, cs, runs = [], [], []
        for hh in range(g):
            z = z_ref[hh]
            z_ref[hh] = zs_next[hh]
            zm = jnp.where(past, z, NEG_BIG) if masked else z
            sp = _softplus2(zm.astype(BF16))
            cs.append(jnp.dot(suffix, sp, preferred_element_type=F32))
            zs.append(z)
        for hh in range(g):
            run = run_ref[hh:hh + 1, :]
            runs.append(run)
            run_ref[hh:hh + 1, :] = run + cs[hh][0:1, :]
        lowest = jnp.min(run_ref[...])
        for hh in range(g):
            w = jnp.exp2(zs[hh] - cs[hh] - runs[hh])
            if masked:
                w = jnp.where(past, w, 0.0)
            vt = vt_ref[j, hh * HEAD_DIM:(hh + 1) * HEAD_DIM, :]
            acc_ref[hh] += jnp.dot(vt, w.astype(BF16), preferred_element_type=F32)
        return lowest

    for hh, z in enumerate(scores(i)):
        z_ref[hh] = z
    acc_ref[...] = jnp.zeros(acc_ref.shape, F32)
    run_ref[...] = jnp.zeros(run_ref.shape, F32)
    lowest = block(i, True)

    def cond(state):
        n, lowest = state
        return jnp.logical_and(n < i, lowest < SB_DEAD_LOG2)

    def body(state):
        n, _ = state
        return n + 1, block(i - 1 - n, False)

    lax.while_loop(cond, body, (jnp.int32(0), lowest))
    for hh in range(g):
        o_ref[hh * HEAD_DIM:(hh + 1) * HEAD_DIM, :] = acc_ref[hh].astype(o_ref.dtype)


def _fox_kernel(qt_ref, augqt_ref, k_ref, augk_ref, vt_ref, o_ref, rhs_ref, s_ref, acc_ref, m_ref,
                *, tq):
    g = N_HEADS
    i = pl.program_id(1)
    _fill_query_operand(rhs_ref, qt_ref, augqt_ref)
    rows = lax.broadcasted_iota(jnp.int32, (tq, tq), 0)
    cols = lax.broadcasted_iota(jnp.int32, (tq, tq), 1)
    causal = rows <= cols

    def scores(j):
        start = pl.multiple_of(j * tq, tq)
        out = []
        for hh in range(g):
            pair = slice((hh // 2) * PAIR, (hh // 2 + 1) * PAIR)
            lhs = jnp.concatenate([k_ref[pl.ds(start, tq), pair], augk_ref[pl.ds(start, tq), :]],
                                  axis=1)
            out.append(jnp.dot(lhs, rhs_ref[hh], preferred_element_type=F32))
        return out

    def block(j, j_next, masked):
        ss_next = scores(j_next)
        alphas, ps = [], []
        for hh in range(g):
            s = s_ref[hh]
            s_ref[hh] = ss_next[hh]
            if masked:
                s = jnp.where(causal, s, NEG_BIG)
            m = m_ref[hh:hh + 1, :]
            m_new = jnp.maximum(m, jnp.max(s, axis=0, keepdims=True))
            alpha = jnp.exp2(m - m_new)
            p = jnp.exp2(s - m_new)
            m_ref[hh:hh + 1, :] = m_new
            alphas.append(alpha)
            ps.append(p.astype(BF16))
        for hh in range(g):
            vt = jnp.concatenate([vt_ref[j, hh * HEAD_DIM:(hh + 1) * HEAD_DIM, :], den_rows], axis=0)
            acc_ref[hh] = alphas[hh] * acc_ref[hh] + jnp.dot(vt, ps[hh],
                                                             preferred_element_type=F32)

    den_rows = jnp.ones((DEN_ROWS, tq), BF16)
    for hh, s in enumerate(scores(i)):
        s_ref[hh] = s
    acc_ref[...] = jnp.zeros(acc_ref.shape, F32)
    m_ref[...] = jnp.full(m_ref.shape, NEG_BIG, F32)
    block(i, 0, True)
    last = jnp.maximum(i - 1, 0)

    @pl.loop(0, i // 2)
    def _(n):
        block(2 * n, 2 * n + 1, False)
        block(2 * n + 1, jnp.minimum(2 * n + 2, last), False)

    @pl.when(i % 2 == 1)
    def _():
        block(last, last, False)

    for hh in range(g):
        o_ref[hh * HEAD_DIM:(hh + 1) * HEAD_DIM, :] = (
            acc_ref[hh, :HEAD_DIM, :] / acc_ref[hh, HEAD_DIM:HEAD_DIM + 1, :]).astype(o_ref.dtype)


def _attn_call(qt, k, vt, tq, aug=None):
    b, d, s = qt.shape
    nk = s // tq
    q_spec = pl.BlockSpec((None, d, tq), lambda bi, i: (bi, 0, i))
    k_spec = pl.BlockSpec((None, s, d), lambda bi, i: (bi, 0, 0))
    vt_spec = pl.BlockSpec((None, nk, d, tq), lambda bi, i: (bi, 0, 0, 0))
    if aug is None:
        body, name, kdim, acc_rows = _sb_kernel, "sb_attn", PAIR, HEAD_DIM
        operands, in_specs = (qt, k, vt), [q_spec, k_spec, vt_spec]
    else:
        augk, augqt = aug
        body, name, kdim, acc_rows = _fox_kernel, "fox_attn", 2 * PAIR, HEAD_DIM + DEN_ROWS
        operands = (qt, augqt, k, augk, vt)
        in_specs = [q_spec, pl.BlockSpec((None, N_HEADS * AUG, tq), lambda bi, i: (bi, 0, i)),
                    k_spec, pl.BlockSpec((None, s, N_HEADS * AUG), lambda bi, i: (bi, 0, 0)),
                    vt_spec]
    return pl.pallas_call(
        functools.partial(body, tq=tq),
        grid=(b, s // tq),
        in_specs=in_specs,
        out_specs=q_spec,
        out_shape=jax.ShapeDtypeStruct((b, d, s), BF16),
        scratch_shapes=[pltpu.VMEM((N_HEADS, kdim, tq), BF16),
                        pltpu.VMEM((N_HEADS, tq, tq), F32),
                        pltpu.VMEM((N_HEADS, acc_rows, tq), F32),
                        pltpu.VMEM((N_HEADS, tq), F32)],
        compiler_params=pltpu.CompilerParams(
            dimension_semantics=("parallel", "arbitrary"), vmem_limit_bytes=V7X_VMEM_LIMIT),
        name=name,
    )(*operands)


def _merge_kernel(x_ref, yst_ref, yft_ref, gate_ref, mod_ref, g_ref, ws_ref, wfx_ref, wo_ref,
                  x1_ref, h2_ref):
    d = x_ref.shape[-1]
    mod = mod_ref[...]
    a = jnp.dot(yst_ref[...].T, ws_ref[...], preferred_element_type=F32)
    b = jnp.dot(yft_ref[...].T, wfx_ref[...], preferred_element_type=F32)
    gate = gate_ref[...].astype(F32)
    merged = gate[:, :d] * a + gate[:, d:] * b
    x1 = x_ref[...] + mod[2:3, :] * jnp.dot(merged.astype(BF16), wo_ref[...],
                                            preferred_element_type=F32)
    x1_ref[...] = x1
    h2_ref[...] = _rms_modulate(x1, g_ref[...], mod[4:5, :], mod[3:4, :]).astype(BF16)


def _merge_call(x, yt_sb, yt_fx, gates, mod3, g_ffn, w_bs, w_bf, w_out, tm):
    b, s, d = x.shape
    row = lambda bi, i: (bi, i, 0)
    yt_spec = pl.BlockSpec((None, D_HEADS, tm), lambda bi, i: (bi, 0, i))
    return pl.pallas_call(
        _merge_kernel,
        grid=(b, s // tm),
        in_specs=[pl.BlockSpec((None, tm, d), row), yt_spec, yt_spec,
                  pl.BlockSpec((None, tm, 2 * d), row),
                  pl.BlockSpec((None, N_MOD, d), lambda bi, i: (bi, 0, 0)),
                  _const_spec((1, d)),
                  _const_spec(w_bs.shape), _const_spec(w_bf.shape), _const_spec(w_out.shape)],
        out_specs=[pl.BlockSpec((None, tm, d), row), pl.BlockSpec((None, tm, d), row)],
        out_shape=[jax.ShapeDtypeStruct((b, s, d), F32), jax.ShapeDtypeStruct((b, s, d), BF16)],
        compiler_params=pltpu.CompilerParams(
            dimension_semantics=("parallel", "parallel"), vmem_limit_bytes=V7X_VMEM_LIMIT),
        name="merge",
    )(x, yt_sb, yt_fx, gates, mod3, g_ffn, w_bs, w_bf, w_out)


def _ffn_kernel(x1_ref, h2_ref, mod_ref, g_ref, wg_ref, wu_ref, wd_ref, o_ref, *, bounds):
    h2 = h2_ref[...]
    ffn = None
    for lo, hi in zip(bounds[:-1], bounds[1:]):
        sl = slice(lo, hi)
        gt = jnp.dot(h2, wg_ref[:, sl], preferred_element_type=F32)
        up = jnp.dot(h2, wu_ref[:, sl], preferred_element_type=F32)
        act = (gt * jax.nn.sigmoid(gt) * up).astype(BF16)
        part = jnp.dot(act, wd_ref[sl, :], preferred_element_type=F32)
        ffn = part if ffn is None else ffn + part
    x2 = x1_ref[...] + mod_ref[5:6, :] * ffn
    ms = jnp.mean(x2 * x2, axis=-1, keepdims=True)
    o_ref[...] = x2 * lax.rsqrt(ms + RMS_EPS) * g_ref[...]


def _ffn_call(x1, h2, mod3, g_final, w_gate, w_up, w_down, tm):
    b, s, d = x1.shape
    row = lambda bi, i: (bi, i, 0)
    dff = w_gate.shape[1]
    half = -(-dff // (2 * V7X_MXU_WIDTH)) * V7X_MXU_WIDTH
    bounds = (0, half, dff) if half < dff else (0, dff)
    return pl.pallas_call(
        functools.partial(_ffn_kernel, bounds=bounds),
        grid=(b, s // tm),
        in_specs=[pl.BlockSpec((None, tm, d), row),
                  pl.BlockSpec((None, tm, d), row),
                  pl.BlockSpec((None, N_MOD, d), lambda bi, i: (bi, 0, 0)),
                  _const_spec((1, d)),
                  _const_spec(w_gate.shape), _const_spec(w_up.shape), _const_spec(w_down.shape)],
        out_specs=pl.BlockSpec((None, tm, d), row),
        out_shape=jax.ShapeDtypeStruct((b, s, d), F32),
        compiler_params=pltpu.CompilerParams(
            dimension_semantics=("parallel", "parallel"), vmem_limit_bytes=V7X_VMEM_LIMIT),
        name="ffn",
    )(x1, h2, mod3, g_final, w_gate, w_up, w_down)


def kernel(x, c, w_ada, b_ada, g_mix, w_in, b_forget, b_gate, w_branch_sb, w_branch_fox,
           w_out, g_ffn, w_ffn_gate, w_ffn_up, w_ffn_down, g_final):
    b, s, d = x.shape
    depth = w_ada.shape[0]
    assert depth == 1, "the final RMSNorm is fused into the last (only) layer's FFN call"
    tm = min(512, s)
    tq = min(256, s)
    dh = D_HEADS
    for l in range(depth):
        mod3 = _mod_call(c, w_ada[l], b_ada[l]).reshape(b, N_MOD, d)
        w = w_in[l]
        w_k = jnp.concatenate([w[:, dh:2 * dh], w[:, 4 * dh:5 * dh]], axis=1).astype(BF16)
        w_qvt = jnp.concatenate([w[:, 0:dh], w[:, 3 * dh:4 * dh], w[:, 2 * dh:3 * dh],
                                 w[:, 5 * dh:6 * dh], w[:, 6 * dh:6 * dh + N_HEADS],
                                 jnp.zeros((d, F_PAD_ROWS - N_HEADS), w.dtype)],
                                axis=1).T.astype(BF16)
        w_g = w[:, 6 * dh + N_HEADS:].astype(BF16)
        k_sb, k_fx, qt_sb, qt_fx, vt_sb, vt_fx, lf_t, gates = _inproj_call(
            x, mod3, g_mix[l].reshape(1, d), w_k, w_qvt, w_g, b_forget[l].reshape(N_HEADS, 1),
            b_gate[l].reshape(1, 2 * d), tm, tq)

        yt_sb = _attn_call(qt_sb, k_sb, vt_sb, tq)
        aug = _cumsum_call(lf_t)
        yt_fx = _attn_call(qt_fx, k_fx, vt_fx, tq, aug)

        x1, h2 = _merge_call(x, yt_sb, yt_fx, gates, mod3, g_ffn[l].reshape(1, d),
                             w_branch_sb[l].astype(BF16), w_branch_fox[l].astype(BF16),
                             w_out[l].astype(BF16), tm)
        x = _ffn_call(x1, h2, mod3, g_final.reshape(1, d), w_ffn_gate[l].astype(BF16),
                      w_ffn_up[l].astype(BF16), w_ffn_down[l].astype(BF16), tm)
    return x
```

```python
import functools
import math

import jax
import jax.numpy as jnp
from jax import lax
from jax.experimental import pallas as pl
from jax.experimental.pallas import tpu as pltpu

F32 = jnp.float32
BF16 = jnp.bfloat16

HEAD_DIM = 64
N_HEADS = 8
D_HEADS = N_HEADS * HEAD_DIM
PAIR = 2 * HEAD_DIM
N_MOD = 6
RMS_EPS = 1e-6
LOG2E = math.log2(math.e)
SB_Q_SCALE = HEAD_DIM ** -0.5
FOX_Q_SCALE = LOG2E * HEAD_DIM ** -0.5
AUG = 16
DEN_ROWS = 16
F_PAD_ROWS = 16
NEG_BIG = -1e30
SB_DEAD = 160.0 * math.log(2.0)
V7X_VMEM_LIMIT = 56 * 1024 * 1024
V7X_MXU_WIDTH = 256
NT_DIMS = (((1,), (1,)), ((), ()))


def _rms_modulate(x, g, scale, shift):
    ms = jnp.mean(x * x, axis=-1, keepdims=True)
    y = x * lax.rsqrt(ms + RMS_EPS) * g
    return y * (1.0 + scale) + shift


def _const_spec(shape):
    return pl.BlockSpec(shape, lambda *_: (0,) * len(shape), pipeline_mode=pl.Buffered(1))


def _mod_kernel(c_ref, w_ref, b_ref, o_ref):
    c = c_ref[...]
    ca = c * jax.nn.sigmoid(c)
    o_ref[...] = jnp.dot(ca, w_ref[...], preferred_element_type=F32,
                         precision=lax.Precision.HIGHEST) + b_ref[...]


def _mod_call(c, w_ada, b_ada):
    b, d = c.shape
    n = w_ada.shape[1]
    tn = 1024
    return pl.pallas_call(
        _mod_kernel,
        grid=(n // tn,),
        in_specs=[pl.BlockSpec((b, d), lambda j: (0, 0)),
                  pl.BlockSpec((d, tn), lambda j: (0, j)),
                  pl.BlockSpec((1, tn), lambda j: (0, j))],
        out_specs=pl.BlockSpec((b, tn), lambda j: (0, j)),
        out_shape=jax.ShapeDtypeStruct((b, n), F32),
        name="mod",
    )(c, w_ada, b_ada.reshape(1, n))


def _inproj_kernel(x_ref, mod_ref, g_ref, wk_ref, wqvt_ref, wg_ref, bf_ref, bg_ref,
                   ks_ref, kf_ref, qts_ref, qtf_ref, vts_ref, vtf_ref, lft_ref, gate_ref, *, tk):
    mod = mod_ref[...]
    h = _rms_modulate(x_ref[...], g_ref[...], mod[1:2, :], mod[0:1, :]).astype(BF16)
    d = D_HEADS
    kk = jnp.dot(h, wk_ref[...], preferred_element_type=F32)
    ks_ref[...] = kk[:, :d].astype(BF16)
    kf_ref[...] = kk[:, d:].astype(BF16)
    t = lax.dot_general(wqvt_ref[...], h, NT_DIMS, preferred_element_type=F32)
    qts_ref[...] = (t[0 * d:1 * d] * SB_Q_SCALE).astype(BF16)
    qtf_ref[...] = (t[1 * d:2 * d] * FOX_Q_SCALE).astype(BF16)
    for c in range(vts_ref.shape[0]):
        vts_ref[c] = t[2 * d:3 * d, c * tk:(c + 1) * tk].astype(BF16)
        vtf_ref[c] = t[3 * d:4 * d, c * tk:(c + 1) * tk].astype(BF16)
    f = t[4 * d:4 * d + N_HEADS] + bf_ref[...]
    lft_ref[...] = -(jnp.maximum(-f, 0.0) + jnp.log(1.0 + jnp.exp(-jnp.abs(f)))) * LOG2E
    gl = jnp.dot(h, wg_ref[...], preferred_element_type=F32) + bg_ref[...]
    gate_ref[...] = jax.nn.sigmoid(gl).astype(BF16)


def _inproj_call(x, mod3, g_mix, w_k, w_qvt, w_g, b_f, b_g, tm, tk):
    b, s, d = x.shape
    row = lambda bi, i: (bi, i, 0)
    col = lambda bi, i: (bi, 0, i)
    k_spec = pl.BlockSpec((None, tm, D_HEADS), row)
    qt_spec = pl.BlockSpec((None, D_HEADS, tm), col)
    vt_spec = pl.BlockSpec((None, tm // tk, D_HEADS, tk), lambda bi, i: (bi, i, 0, 0))
    return pl.pallas_call(
        functools.partial(_inproj_kernel, tk=tk),
        grid=(b, s // tm),
        in_specs=[pl.BlockSpec((None, tm, d), row),
                  pl.BlockSpec((None, N_MOD, d), lambda bi, i: (bi, 0, 0)),
                  _const_spec((1, d)),
                  _const_spec(w_k.shape), _const_spec(w_qvt.shape),
                  _const_spec(w_g.shape), _const_spec(b_f.shape), _const_spec(b_g.shape)],
        out_specs=[k_spec, k_spec, qt_spec, qt_spec, vt_spec, vt_spec,
                   pl.BlockSpec((None, N_HEADS, tm), col),
                   pl.BlockSpec((None, tm, 2 * d), row)],
        out_shape=[jax.ShapeDtypeStruct((b, s, D_HEADS), BF16)] * 2
        + [jax.ShapeDtypeStruct((b, D_HEADS, s), BF16)] * 2
        + [jax.ShapeDtypeStruct((b, s // tk, D_HEADS, tk), BF16)] * 2
        + [jax.ShapeDtypeStruct((b, N_HEADS, s), F32),
           jax.ShapeDtypeStruct((b, s, 2 * d), BF16)],
        compiler_params=pltpu.CompilerParams(
            dimension_semantics=("parallel", "parallel"), vmem_limit_bytes=V7X_VMEM_LIMIT),
        name="inproj",
    )(x, mod3, g_mix, w_k, w_qvt, w_g, b_f, b_g)


def _split3(v):
    hi = v.astype(BF16).astype(F32)
    r = v - hi
    mid = r.astype(BF16).astype(F32)
    lo = (r - mid).astype(BF16).astype(F32)
    return hi, mid, lo


def _cumsum_kernel(lft_ref, augk_ref, augqt_ref, *, chunk):
    h, s = lft_ref.shape
    n3 = 3 * h
    r = lax.broadcasted_iota(jnp.int32, (chunk, chunk), 0)
    c = lax.broadcasted_iota(jnp.int32, (chunk, chunk), 1)
    upper = (r <= c).astype(BF16)

    qr = lax.broadcasted_iota(jnp.int32, (h * AUG, n3), 0)
    qc = lax.broadcasted_iota(jnp.int32, (h * AUG, n3), 1)
    qrow = lax.broadcasted_iota(jnp.int32, (h * AUG, 1), 0)
    place_q = jnp.zeros((h * AUG, n3), F32)
    place_k = jnp.zeros((h * AUG, n3), F32)
    ones_q = jnp.zeros((h * AUG, 1), F32)
    ones_k = jnp.zeros((h * AUG, 1), F32)
    for hh in range(h):
        for p in range(3):
            place_q = jnp.where((qr == hh * AUG + 3 + p) & (qc == p * h + hh), 1.0, place_q)
            place_k = jnp.where((qr == hh * AUG + p) & (qc == p * h + hh), -1.0, place_k)
            ones_q = jnp.where(qrow == hh * AUG + p, 1.0, ones_q)
            ones_k = jnp.where(qrow == hh * AUG + 3 + p, 1.0, ones_k)
    place_q = place_q.astype(BF16)
    place_k = place_k.astype(BF16)

    carry = jnp.zeros((h, 1), F32)
    for ci in range(s // chunk):
        sl = slice(ci * chunk, (ci + 1) * chunk)
        pieces = jnp.concatenate(_split3(lft_ref[:, sl]), axis=0).astype(BF16)
        p3 = jnp.dot(pieces, upper, preferred_element_type=F32)
        cum = (p3[0:h] + p3[h:2 * h]) + p3[2 * h:n3] + carry
        carry = cum[:, chunk - 1:chunk]
        stacked = jnp.concatenate(_split3(cum), axis=0).astype(BF16)
        augqt_ref[:, sl] = (jnp.dot(place_q, stacked, preferred_element_type=F32)
                            + ones_q).astype(BF16)
        augkt = jnp.dot(place_k, stacked, preferred_element_type=F32) + ones_k
        augk_ref[sl, :] = augkt.T.astype(BF16)


def _cumsum_call(lf_t):
    b, h, s = lf_t.shape
    return pl.pallas_call(
        functools.partial(_cumsum_kernel, chunk=min(256, s)),
        grid=(b,),
        in_specs=[pl.BlockSpec((None, h, s), lambda bi: (bi, 0, 0))],
        out_specs=[pl.BlockSpec((None, s, h * AUG), lambda bi: (bi, 0, 0)),
                   pl.BlockSpec((None, h * AUG, s), lambda bi: (bi, 0, 0))],
        out_shape=[jax.ShapeDtypeStruct((b, s, h * AUG), BF16),
                   jax.ShapeDtypeStruct((b, h * AUG, s), BF16)],
        compiler_params=pltpu.CompilerParams(dimension_semantics=("parallel",)),
        name="cumsum",
    )(lf_t)


def _softplus(z):
    return jnp.maximum(z, 0.0) + jnp.log(1.0 + jnp.exp(-jnp.abs(z)))


def _fill_query_operand(rhs_ref, qt_ref, augqt_ref):
    rhs_ref[...] = jnp.zeros(rhs_ref.shape, rhs_ref.dtype)
    for hh in range(N_HEADS):
        half = hh % 2
        rhs_ref[hh, half * HEAD_DIM:(half + 1) * HEAD_DIM, :] = (
            qt_ref[hh * HEAD_DIM:(hh + 1) * HEAD_DIM, :])
        if augqt_ref is not None:
            rhs_ref[hh, PAIR + hh * AUG:PAIR + (hh + 1) * AUG, :] = (
                augqt_ref[hh * AUG:(hh + 1) * AUG, :])


def _sb_kernel(qt_ref, k_ref, vt_ref, o_ref, rhs_ref, z_ref, acc_ref, run_ref, *, tq):
    g = N_HEADS
    i = pl.program_id(1)
    _fill_query_operand(rhs_ref, qt_ref, None)
    rows = lax.broadcasted_iota(jnp.int32, (tq, tq), 0)
    cols = lax.broadcasted_iota(jnp.int32, (tq, tq), 1)
    suffix = (cols >= rows).astype(BF16)
    past = rows < cols

    def scores(j):
        start = pl.multiple_of(j * tq, tq)
        return [jnp.dot(k_ref[pl.ds(start, tq), (hh // 2) * PAIR:(hh // 2 + 1) * PAIR], rhs_ref[hh],
                        preferred_element_type=F32) for hh in range(g)]

    def block(j, masked):
        zs_next = scores(jnp.maximum(j - 1, 0))
        zs, cs, runs = [], [], []
        for hh in range(g):
            z = z_ref[hh]
            z_ref[hh] = zs_next[hh]
            zm = jnp.where(past, z, NEG_BIG) if masked else z
            sp = _softplus(zm.astype(BF16))
            cs.append(jnp.dot(suffix, sp, preferred_element_type=F32))
            zs.append(z)
        for hh in range(g):
            run = run_ref[hh:hh + 1, :]
            runs.append(run)
            run_ref[hh:hh + 1, :] = run + cs[hh][0:1, :]
        lowest = jnp.min(run_ref[...])
        for hh in range(g):
            w = jnp.exp(zs[hh] - cs[hh] - runs[hh])
            if masked:
                w = jnp.where(past, w, 0.0)
            vt = vt_ref[j, hh * HEAD_DIM:(hh + 1) * HEAD_DIM, :]
            acc_ref[hh] += jnp.dot(vt, w.astype(BF16), preferred_element_type=F32)
        return lowest

    for hh, z in enumerate(scores(i)):
        z_ref[hh] = z
    acc_ref[...] = jnp.zeros(acc_ref.shape, F32)
    run_ref[...] = jnp.zeros(run_ref.shape, F32)
    lowest = block(i, True)

    def cond(state):
        n, lowest = state
        return jnp.logical_and(n < i, lowest < SB_DEAD)

    def body(state):
        n, _ = state
        return n + 1, block(i - 1 - n, False)

    lax.while_loop(cond, body, (jnp.int32(0), lowest))
    for hh in range(g):
        o_ref[hh * HEAD_DIM:(hh + 1) * HEAD_DIM, :] = acc_ref[hh].astype(o_ref.dtype)


def _fox_kernel(qt_ref, augqt_ref, k_ref, augk_ref, vt_ref, o_ref, rhs_ref, s_ref, acc_ref, m_ref,
                *, tq):
    g = N_HEADS
    i = pl.program_id(1)
    _fill_query_operand(rhs_ref, qt_ref, augqt_ref)
    rows = lax.broadcasted_iota(jnp.int32, (tq, tq), 0)
    cols = lax.broadcasted_iota(jnp.int32, (tq, tq), 1)
    causal = rows <= cols

    def scores(j):
        start = pl.multiple_of(j * tq, tq)
        out = []
        for hh in range(g):
            pair = slice((hh // 2) * PAIR, (hh // 2 + 1) * PAIR)
            lhs = jnp.concatenate([k_ref[pl.ds(start, tq), pair], augk_ref[pl.ds(start, tq), :]],
                                  axis=1)
            out.append(jnp.dot(lhs, rhs_ref[hh], preferred_element_type=F32))
        return out

    def block(j, j_next, masked):
        ss_next = scores(j_next)
        alphas, ps = [], []
        for hh in range(g):
            s = s_ref[hh]
            s_ref[hh] = ss_next[hh]
            if masked:
                s = jnp.where(causal, s, NEG_BIG)
            m = m_ref[hh:hh + 1, :]
            m_new = jnp.maximum(m, jnp.max(s, axis=0, keepdims=True))
            alpha = jnp.exp2(m - m_new)
            p = jnp.exp2(s - m_new)
            m_ref[hh:hh + 1, :] = m_new
            alphas.append(alpha)
            ps.append(p.astype(BF16))
        for hh in range(g):
            vt = jnp.concatenate([vt_ref[j, hh * HEAD_DIM:(hh + 1) * HEAD_DIM, :], den_rows], axis=0)
            acc_ref[hh] = alphas[hh] * acc_ref[hh] + jnp.dot(vt, ps[hh],
                                                             preferred_element_type=F32)

    den_rows = jnp.ones((DEN_ROWS, tq), BF16)
    for hh, s in enumerate(scores(i)):
        s_ref[hh] = s
    acc_ref[...] = jnp.zeros(acc_ref.shape, F32)
    m_ref[...] = jnp.full(m_ref.shape, NEG_BIG, F32)
    block(i, 0, True)
    last = jnp.maximum(i - 1, 0)

    @pl.loop(0, i // 2)
    def _(n):
        block(2 * n, 2 * n + 1, False)
        block(2 * n + 1, jnp.minimum(2 * n + 2, last), False)

    @pl.when(i % 2 == 1)
    def _():
        block(last, last, False)

    for hh in range(g):
        o_ref[hh * HEAD_DIM:(hh + 1) * HEAD_DIM, :] = (
            acc_ref[hh, :HEAD_DIM, :] / acc_ref[hh, HEAD_DIM:HEAD_DIM + 1, :]).astype(o_ref.dtype)


def _attn_call(qt, k, vt, tq, aug=None):
    b, d, s = qt.shape
    nk = s // tq
    q_spec = pl.BlockSpec((None, d, tq), lambda bi, i: (bi, 0, i))
    k_spec = pl.BlockSpec((None, s, d), lambda bi, i: (bi, 0, 0))
    vt_spec = pl.BlockSpec((None, nk, d, tq), lambda bi, i: (bi, 0, 0, 0))
    if aug is None:
        body, name, kdim, acc_rows = _sb_kernel, "sb_attn", PAIR, HEAD_DIM
        operands, in_specs = (qt, k, vt), [q_spec, k_spec, vt_spec]
    else:
        augk, augqt = aug
        body, name, kdim, acc_rows = _fox_kernel, "fox_attn", 2 * PAIR, HEAD_DIM + DEN_ROWS
        operands = (qt, augqt, k, augk, vt)
        in_specs = [q_spec, pl.BlockSpec((None, N_HEADS * AUG, tq), lambda bi, i: (bi, 0, i)),
                    k_spec, pl.BlockSpec((None, s, N_HEADS * AUG), lambda bi, i: (bi, 0, 0)),
                    vt_spec]
    return pl.pallas_call(
        functools.partial(body, tq=tq),
        grid=(b, s // tq),
        in_specs=in_specs,
        out_specs=q_spec,
        out_shape=jax.ShapeDtypeStruct((b, d, s), BF16),
        scratch_shapes=[pltpu.VMEM((N_HEADS, kdim, tq), BF16),
                        pltpu.VMEM((N_HEADS, tq, tq), F32),
                        pltpu.VMEM((N_HEADS, acc_rows, tq), F32),
                        pltpu.VMEM((N_HEADS, tq), F32)],
        compiler_params=pltpu.CompilerParams(
            dimension_semantics=("parallel", "arbitrary"), vmem_limit_bytes=V7X_VMEM_LIMIT),
        name=name,
    )(*operands)


def _merge_kernel(x_ref, yst_ref, yft_ref, gate_ref, mod_ref, g_ref, ws_ref, wfx_ref, wo_ref,
                  x1_ref, h2_ref):
    d = x_ref.shape[-1]
    mod = mod_ref[...]
    a = jnp.dot(yst_ref[...].T, ws_ref[...], preferred_element_type=F32)
    b = jnp.dot(yft_ref[...].T, wfx_ref[...], preferred_element_type=F32)
    gate = gate_ref[...].astype(F32)
    merged = gate[:, :d] * a + gate[:, d:] * b
    x1 = x_ref[...] + mod[2:3, :] * jnp.dot(merged.astype(BF16), wo_ref[...],
                                            preferred_element_type=F32)
    x1_ref[...] = x1
    h2_ref[...] = _rms_modulate(x1, g_ref[...], mod[4:5, :], mod[3:4, :]).astype(BF16)


def _merge_call(x, yt_sb, yt_fx, gates, mod3, g_ffn, w_bs, w_bf, w_out, tm):
    b, s, d = x.shape
    row = lambda bi, i: (bi, i, 0)
    yt_spec = pl.BlockSpec((None, D_HEADS, tm), lambda bi, i: (bi, 0, i))
    return pl.pallas_call(
        _merge_kernel,
        grid=(b, s // tm),
        in_specs=[pl.BlockSpec((None, tm, d), row), yt_spec, yt_spec,
                  pl.BlockSpec((None, tm, 2 * d), row),
                  pl.BlockSpec((None, N_MOD, d), lambda bi, i: (bi, 0, 0)),
                  _const_spec((1, d)),
                  _const_spec(w_bs.shape), _const_spec(w_bf.shape), _const_spec(w_out.shape)],
        out_specs=[pl.BlockSpec((None, tm, d), row), pl.BlockSpec((None, tm, d), row)],
        out_shape=[jax.ShapeDtypeStruct((b, s, d), F32), jax.ShapeDtypeStruct((b, s, d), BF16)],
        compiler_params=pltpu.CompilerParams(
            dimension_semantics=("parallel", "parallel"), vmem_limit_bytes=V7X_VMEM_LIMIT),
        name="merge",
    )(x, yt_sb, yt_fx, gates, mod3, g_ffn, w_bs, w_bf, w_out)


def _ffn_kernel(x1_ref, h2_ref, mod_ref, g_ref, wg_ref, wu_ref, wd_ref, o_ref, *, bounds):
    h2 = h2_ref[...]
    ffn = None
    for lo, hi in zip(bounds[:-1], bounds[1:]):
        sl = slice(lo, hi)
        gt = jnp.dot(h2, wg_ref[:, sl], preferred_element_type=F32)
        up = jnp.dot(h2, wu_ref[:, sl], preferred_element_type=F32)
        act = (gt * jax.nn.sigmoid(gt) * up).astype(BF16)
        part = jnp.dot(act, wd_ref[sl, :], preferred_element_type=F32)
        ffn = part if ffn is None else ffn + part
    x2 = x1_ref[...] + mod_ref[5:6, :] * ffn
    ms = jnp.mean(x2 * x2, axis=-1, keepdims=True)
    o_ref[...] = x2 * lax.rsqrt(ms + RMS_EPS) * g_ref[...]


def _ffn_call(x1, h2, mod3, g_final, w_gate, w_up, w_down, tm):
    b, s, d = x1.shape
    row = lambda bi, i: (bi, i, 0)
    dff = w_gate.shape[1]
    half = -(-dff // (2 * V7X_MXU_WIDTH)) * V7X_MXU_WIDTH
    bounds = (0, half, dff) if half < dff else (0, dff)
    return pl.pallas_call(
        functools.partial(_ffn_kernel, bounds=bounds),
        grid=(b, s // tm),
        in_specs=[pl.BlockSpec((None, tm, d), row),
                  pl.BlockSpec((None, tm, d), row),
                  pl.BlockSpec((None, N_MOD, d), lambda bi, i: (bi, 0, 0)),
                  _const_spec((1, d)),
                  _const_spec(w_gate.shape), _const_spec(w_up.shape), _const_spec(w_down.shape)],
        out_specs=pl.BlockSpec((None, tm, d), row),
        out_shape=jax.ShapeDtypeStruct((b, s, d), F32),
        compiler_params=pltpu.CompilerParams(
            dimension_semantics=("parallel", "parallel"), vmem_limit_bytes=V7X_VMEM_LIMIT),
        name="ffn",
    )(x1, h2, mod3, g_final, w_gate, w_up, w_down)


def kernel(x, c, w_ada, b_ada, g_mix, w_in, b_forget, b_gate, w_branch_sb, w_branch_fox,
           w_out, g_ffn, w_ffn_gate, w_ffn_up, w_ffn_down, g_final):
    b, s, d = x.shape
    depth = w_ada.shape[0]
    assert depth == 1, "the final RMSNorm is fused into the last (only) layer's FFN call"
    tm = min(512, s)
    tq = min(256, s)
    dh = D_HEADS
    for l in range(depth):
        mod3 = _mod_call(c, w_ada[l], b_ada[l]).reshape(b, N_MOD, d)
        w = w_in[l]
        w_k = jnp.concatenate([w[:, dh:2 * dh], w[:, 4 * dh:5 * dh]], axis=1).astype(BF16)
        w_qvt = jnp.concatenate([w[:, 0:dh], w[:, 3 * dh:4 * dh], w[:, 2 * dh:3 * dh],
                                 w[:, 5 * dh:6 * dh], w[:, 6 * dh:6 * dh + N_HEADS],
                                 jnp.zeros((d, F_PAD_ROWS - N_HEADS), w.dtype)],
                                axis=1).T.astype(BF16)
        w_g = w[:, 6 * dh + N_HEADS:].astype(BF16)
        k_sb, k_fx, qt_sb, qt_fx, vt_sb, vt_fx, lf_t, gates = _inproj_call(
            x, mod3, g_mix[l].reshape(1, d), w_k, w_qvt, w_g, b_forget[l].reshape(N_HEADS, 1),
            b_gate[l].reshape(1, 2 * d), tm, tq)

        yt_sb = _attn_call(qt_sb, k_sb, vt_sb, tq)
        aug = _cumsum_call(lf_t)
        yt_fx = _attn_call(qt_fx, k_fx, vt_fx, tq, aug)

        x1, h2 = _merge_call(x, yt_sb, yt_fx, gates, mod3, g_ffn[l].reshape(1, d),
                             w_branch_sb[l].astype(BF16), w_branch_fox[l].astype(BF16),
                             w_out[l].astype(BF16), tm)
        x = _ffn_call(x1, h2, mod3, g_final.reshape(1, d), w_ffn_gate[l].astype(BF16),
                      w_ffn_up[l].astype(BF16), w_ffn_down[l].astype(BF16), tm)
    return x
```

```python
import functools
import math

import jax
import jax.numpy as jnp
from jax import lax
from jax.experimental import pallas as pl
from jax.experimental.pallas import tpu as pltpu

F32 = jnp.float32
BF16 = jnp.bfloat16

HEAD_DIM = 64
N_HEADS = 8
D_HEADS = N_HEADS * HEAD_DIM
PAIR = 2 * HEAD_DIM
N_MOD = 6
RMS_EPS = 1e-6
LOG2E = math.log2(math.e)
SB_Q_SCALE = HEAD_DIM ** -0.5
FOX_Q_SCALE = LOG2E * HEAD_DIM ** -0.5
AUG = 16
DEN_ROWS = 16
F_PAD_ROWS = 16
NEG_BIG = -1e30
SB_DEAD = 160.0 * math.log(2.0)
V7X_VMEM_LIMIT = 56 * 1024 * 1024
V7X_MXU_WIDTH = 256
NT_DIMS = (((1,), (1,)), ((), ()))


def _rms_modulate(x, g, scale, shift):
    ms = jnp.mean(x * x, axis=-1, keepdims=True)
    y = x * lax.rsqrt(ms + RMS_EPS) * g
    return y * (1.0 + scale) + shift


def _const_spec(shape):
    return pl.BlockSpec(shape, lambda *_: (0,) * len(shape), pipeline_mode=pl.Buffered(1))


def _mod_kernel(c_ref, w_ref, b_ref, o_ref):
    c = c_ref[...]
    ca = c * jax.nn.sigmoid(c)
    o_ref[...] = jnp.dot(ca, w_ref[...], preferred_element_type=F32,
                         precision=lax.Precision.HIGHEST) + b_ref[...]


def _mod_call(c, w_ada, b_ada):
    b, d = c.shape
    n = w_ada.shape[1]
    tn = 1024
    return pl.pallas_call(
        _mod_kernel,
        grid=(n // tn,),
        in_specs=[pl.BlockSpec((b, d), lambda j: (0, 0)),
                  pl.BlockSpec((d, tn), lambda j: (0, j)),
                  pl.BlockSpec((1, tn), lambda j: (0, j))],
        out_specs=pl.BlockSpec((b, tn), lambda j: (0, j)),
        out_shape=jax.ShapeDtypeStruct((b, n), F32),
        name="mod",
    )(c, w_ada, b_ada.reshape(1, n))


def _inproj_kernel(x_ref, mod_ref, g_ref, wk_ref, wqvt_ref, wg_ref, bf_ref, bg_ref,
                   ks_ref, kf_ref, qts_ref, qtf_ref, vts_ref, vtf_ref, lft_ref, gate_ref, *, tk):
    mod = mod_ref[...]
    h = _rms_modulate(x_ref[...], g_ref[...], mod[1:2, :], mod[0:1, :]).astype(BF16)
    d = D_HEADS
    kk = jnp.dot(h, wk_ref[...], preferred_element_type=F32)
    ks_ref[...] = kk[:, :d].astype(BF16)
    kf_ref[...] = kk[:, d:].astype(BF16)
    t = lax.dot_general(wqvt_ref[...], h, NT_DIMS, preferred_element_type=F32)
    qts_ref[...] = (t[0 * d:1 * d] * SB_Q_SCALE).astype(BF16)
    qtf_ref[...] = (t[1 * d:2 * d] * FOX_Q_SCALE).astype(BF16)
    for c in range(vts_ref.shape[0]):
        vts_ref[c] = t[2 * d:3 * d, c * tk:(c + 1) * tk].astype(BF16)
        vtf_ref[c] = t[3 * d:4 * d, c * tk:(c + 1) * tk].astype(BF16)
    f = t[4 * d:4 * d + N_HEADS] + bf_ref[...]
    lft_ref[...] = -(jnp.maximum(-f, 0.0) + jnp.log(1.0 + jnp.exp(-jnp.abs(f)))) * LOG2E
    gl = jnp.dot(h, wg_ref[...], preferred_element_type=F32) + bg_ref[...]
    gate_ref[...] = jax.nn.sigmoid(gl).astype(BF16)


def _inproj_call(x, mod3, g_mix, w_k, w_qvt, w_g, b_f, b_g, tm, tk):
    b, s, d = x.shape
    row = lambda bi, i: (bi, i, 0)
    col = lambda bi, i: (bi, 0, i)
    k_spec = pl.BlockSpec((None, tm, D_HEADS), row)
    qt_spec = pl.BlockSpec((None, D_HEADS, tm), col)
    vt_spec = pl.BlockSpec((None, tm // tk, D_HEADS, tk), lambda bi, i: (bi, i, 0, 0))
    return pl.pallas_call(
        functools.partial(_inproj_kernel, tk=tk),
        grid=(b, s // tm),
        in_specs=[pl.BlockSpec((None, tm, d), row),
                  pl.BlockSpec((None, N_MOD, d), lambda bi, i: (bi, 0, 0)),
                  _const_spec((1, d)),
                  _const_spec(w_k.shape), _const_spec(w_qvt.shape),
                  _const_spec(w_g.shape), _const_spec(b_f.shape), _const_spec(b_g.shape)],
        out_specs=[k_spec, k_spec, qt_spec, qt_spec, vt_spec, vt_spec,
                   pl.BlockSpec((None, N_HEADS, tm), col),
                   pl.BlockSpec((None, tm, 2 * d), row)],
        out_shape=[jax.ShapeDtypeStruct((b, s, D_HEADS), BF16)] * 2
        + [jax.ShapeDtypeStruct((b, D_HEADS, s), BF16)] * 2
        + [jax.ShapeDtypeStruct((b, s // tk, D_HEADS, tk), BF16)] * 2
        + [jax.ShapeDtypeStruct((b, N_HEADS, s), F32),
           jax.ShapeDtypeStruct((b, s, 2 * d), BF16)],
        compiler_params=pltpu.CompilerParams(
            dimension_semantics=("parallel", "parallel"), vmem_limit_bytes=V7X_VMEM_LIMIT),
        name="inproj",
    )(x, mod3, g_mix, w_k, w_qvt, w_g, b_f, b_g)


def _split3(v):
    hi = v.astype(BF16).astype(F32)
    r = v - hi
    mid = r.astype(BF16).astype(F32)
    lo = (r - mid).astype(BF16).astype(F32)
    return hi, mid, lo


def _cumsum_kernel(lft_ref, augk_ref, augqt_ref, *, chunk):
    h, s = lft_ref.shape
    n3 = 3 * h
    r = lax.broadcasted_iota(jnp.int32, (chunk, chunk), 0)
    c = lax.broadcasted_iota(jnp.int32, (chunk, chunk), 1)
    upper = (r <= c).astype(BF16)

    qr = lax.broadcasted_iota(jnp.int32, (h * AUG, n3), 0)
    qc = lax.broadcasted_iota(jnp.int32, (h * AUG, n3), 1)
    qrow = lax.broadcasted_iota(jnp.int32, (h * AUG, 1), 0)
    place_q = jnp.zeros((h * AUG, n3), F32)
    place_k = jnp.zeros((h * AUG, n3), F32)
    ones_q = jnp.zeros((h * AUG, 1), F32)
    ones_k = jnp.zeros((h * AUG, 1), F32)
    for hh in range(h):
        for p in range(3):
            place_q = jnp.where((qr == hh * AUG + 3 + p) & (qc == p * h + hh), 1.0, place_q)
            place_k = jnp.where((qr == hh * AUG + p) & (qc == p * h + hh), -1.0, place_k)
            ones_q = jnp.where(qrow == hh * AUG + p, 1.0, ones_q)
            ones_k = jnp.where(qrow == hh * AUG + 3 + p, 1.0, ones_k)
    place_q = place_q.astype(BF16)
    place_k = place_k.astype(BF16)

    carry = jnp.zeros((h, 1), F32)
    for ci in range(s // chunk):
        sl = slice(ci * chunk, (ci + 1) * chunk)
        pieces = jnp.concatenate(_split3(lft_ref[:, sl]), axis=0).astype(BF16)
        p3 = jnp.dot(pieces, upper, preferred_element_type=F32)
        cum = (p3[0:h] + p3[h:2 * h]) + p3[2 * h:n3] + carry
        carry = cum[:, chunk - 1:chunk]
        stacked = jnp.concatenate(_split3(cum), axis=0).astype(BF16)
        augqt_ref[:, sl] = (jnp.dot(place_q, stacked, preferred_element_type=F32)
                            + ones_q).astype(BF16)
        augkt = jnp.dot(place_k, stacked, preferred_element_type=F32) + ones_k
        augk_ref[sl, :] = augkt.T.astype(BF16)


def _cumsum_call(lf_t):
    b, h, s = lf_t.shape
    return pl.pallas_call(
        functools.partial(_cumsum_kernel, chunk=min(256, s)),
        grid=(b,),
        in_specs=[pl.BlockSpec((None, h, s), lambda bi: (bi, 0, 0))],
        out_specs=[pl.BlockSpec((None, s, h * AUG), lambda bi: (bi, 0, 0)),
                   pl.BlockSpec((None, h * AUG, s), lambda bi: (bi, 0, 0))],
        out_shape=[jax.ShapeDtypeStruct((b, s, h * AUG), BF16),
                   jax.ShapeDtypeStruct((b, h * AUG, s), BF16)],
        compiler_params=pltpu.CompilerParams(dimension_semantics=("parallel",)),
        name="cumsum",
    )(lf_t)


def _softplus(z):
    return jnp.maximum(z, 0.0) + jnp.log(1.0 + jnp.exp(-jnp.abs(z)))


def _fill_query_operand(rhs_ref, qt_ref, augqt_ref, first_tile):
    @pl.when(first_tile)
    def _():
        rhs_ref[...] = jnp.zeros(rhs_ref.shape, rhs_ref.dtype)

    for hh in range(N_HEADS):
        half = hh % 2
        rhs_ref[hh, half * HEAD_DIM:(half + 1) * HEAD_DIM, :] = (
            qt_ref[hh * HEAD_DIM:(hh + 1) * HEAD_DIM, :])
        if augqt_ref is not None:
            rhs_ref[hh, PAIR + hh * AUG:PAIR + (hh + 1) * AUG, :] = (
                augqt_ref[hh * AUG:(hh + 1) * AUG, :])


def _sb_kernel(qt_ref, k_ref, vt_ref, o_ref, rhs_ref, z_ref, acc_ref, run_ref, *, tq):
    g = N_HEADS
    i = pl.program_id(1)
    _fill_query_operand(rhs_ref, qt_ref, None, i == 0)
    rows = lax.broadcasted_iota(jnp.int32, (tq, tq), 0)
    cols = lax.broadcasted_iota(jnp.int32, (tq, tq), 1)
    suffix = (cols >= rows).astype(BF16)
    past = rows < cols

    def scores(j):
        start = pl.multiple_of(j * tq, tq)
        return [jnp.dot(k_ref[pl.ds(start, tq), (hh // 2) * PAIR:(hh // 2 + 1) * PAIR], rhs_ref[hh],
                        preferred_element_type=F32) for hh in range(g)]

    def block(j, masked):
        zs_next = scores(jnp.maximum(j - 1, 0))
        zs, cs, runs = [], [], []
        for hh in range(g):
            z = z_ref[hh]
            z_ref[hh] = zs_next[hh]
            zm = jnp.where(past, z, NEG_BIG) if masked else z
            sp = _softplus(zm.astype(BF16))
            cs.append(jnp.dot(suffix, sp, preferred_element_type=F32))
            zs.append(z)
        for hh in range(g):
            run = run_ref[hh:hh + 1, :]
            runs.append(run)
            run_ref[hh:hh + 1, :] = run + cs[hh][0:1, :]
        lowest = jnp.min(run_ref[...])
        for hh in range(g):
            w = jnp.exp(zs[hh] - cs[hh] - runs[hh])
            if masked:
                w = jnp.where(past, w, 0.0)
            vt = vt_ref[j, hh * HEAD_DIM:(hh + 1) * HEAD_DIM, :]
            acc_ref[hh] += jnp.dot(vt, w.astype(BF16), preferred_element_type=F32)
        return lowest

    for hh, z in enumerate(scores(i)):
        z_ref[hh] = z
    acc_ref[...] = jnp.zeros(acc_ref.shape, F32)
    run_ref[...] = jnp.zeros(run_ref.shape, F32)
    lowest = block(i, True)

    def cond(state):
        n, lowest = state
        return jnp.logical_and(n < i, lowest < SB_DEAD)

    def body(state):
        n, _ = state
        return n + 1, block(i - 1 - n, False)

    lax.while_loop(cond, body, (jnp.int32(0), lowest))
    for hh in range(g):
        o_ref[hh * HEAD_DIM:(hh + 1) * HEAD_DIM, :] = acc_ref[hh].astype(o_ref.dtype)


def _fox_kernel(qt_ref, augqt_ref, k_ref, augk_ref, vt_ref, o_ref, rhs_ref, s_ref, acc_ref, m_ref,
                *, tq):
    g = N_HEADS
    i = pl.program_id(1)
    _fill_query_operand(rhs_ref, qt_ref, augqt_ref, i == 0)
    rows = lax.broadcasted_iota(jnp.int32, (tq, tq), 0)
    cols = lax.broadcasted_iota(jnp.int32, (tq, tq), 1)
    causal = rows <= cols

    def scores(j):
        start = pl.multiple_of(j * tq, tq)
        out = []
        for hh in range(g):
            pair = slice((hh // 2) * PAIR, (hh // 2 + 1) * PAIR)
            lhs = jnp.concatenate([k_ref[pl.ds(start, tq), pair], augk_ref[pl.ds(start, tq), :]],
                                  axis=1)
            out.append(jnp.dot(lhs, rhs_ref[hh], preferred_element_type=F32))
        return out

    def block(j, j_next, masked):
        ss_next = scores(j_next)
        alphas, ps = [], []
        for hh in range(g):
            s = s_ref[hh]
            s_ref[hh] = ss_next[hh]
            if masked:
                s = jnp.where(causal, s, NEG_BIG)
            m = m_ref[hh:hh + 1, :]
            m_new = jnp.maximum(m, jnp.max(s, axis=0, keepdims=True))
            alpha = jnp.exp2(m - m_new)
            p = jnp.exp2(s - m_new)
            m_ref[hh:hh + 1, :] = m_new
            alphas.append(alpha)
            ps.append(p.astype(BF16))
        for hh in range(g):
            vt = jnp.concatenate([vt_ref[j, hh * HEAD_DIM:(hh + 1) * HEAD_DIM, :], den_rows], axis=0)
            acc_ref[hh] = alphas[hh] * acc_ref[hh] + jnp.dot(vt, ps[hh],
                                                             preferred_element_type=F32)

    den_rows = jnp.ones((DEN_ROWS, tq), BF16)
    for hh, s in enumerate(scores(i)):
        s_ref[hh] = s
    acc_ref[...] = jnp.zeros(acc_ref.shape, F32)
    m_ref[...] = jnp.full(m_ref.shape, NEG_BIG, F32)
    block(i, 0, True)
    last = jnp.maximum(i - 1, 0)

    @pl.loop(0, i // 2)
    def _(n):
        block(2 * n, 2 * n + 1, False)
        block(2 * n + 1, jnp.minimum(2 * n + 2, last), False)

    @pl.when(i % 2 == 1)
    def _():
        block(last, last, False)

    for hh in range(g):
        o_ref[hh * HEAD_DIM:(hh + 1) * HEAD_DIM, :] = (
            acc_ref[hh, :HEAD_DIM, :] / acc_ref[hh, HEAD_DIM:HEAD_DIM + 1, :]).astype(o_ref.dtype)


def _attn_call(qt, k, vt, tq, aug=None):
    b, d, s = qt.shape
    nk = s // tq
    q_spec = pl.BlockSpec((None, d, tq), lambda bi, i: (bi, 0, i))
    k_spec = pl.BlockSpec((None, s, d), lambda bi, i: (bi, 0, 0))
    vt_spec = pl.BlockSpec((None, nk, d, tq), lambda bi, i: (bi, 0, 0, 0))
    if aug is None:
        body, name, kdim, acc_rows = _sb_kernel, "sb_attn", PAIR, HEAD_DIM
        operands, in_specs = (qt, k, vt), [q_spec, k_spec, vt_spec]
    else:
        augk, augqt = aug
        body, name, kdim, acc_rows = _fox_kernel, "fox_attn", 2 * PAIR, HEAD_DIM + DEN_ROWS
        operands = (qt, augqt, k, augk, vt)
        in_specs = [q_spec, pl.BlockSpec((None, N_HEADS * AUG, tq), lambda bi, i: (bi, 0, i)),
                    k_spec, pl.BlockSpec((None, s, N_HEADS * AUG), lambda bi, i: (bi, 0, 0)),
                    vt_spec]
    return pl.pallas_call(
        functools.partial(body, tq=tq),
        grid=(b, s // tq),
        in_specs=in_specs,
        out_specs=q_spec,
        out_shape=jax.ShapeDtypeStruct((b, d, s), BF16),
        scratch_shapes=[pltpu.VMEM((N_HEADS, kdim, tq), BF16),
                        pltpu.VMEM((N_HEADS, tq, tq), F32),
                        pltpu.VMEM((N_HEADS, acc_rows, tq), F32),
                        pltpu.VMEM((N_HEADS, tq), F32)],
        compiler_params=pltpu.CompilerParams(
            dimension_semantics=("parallel", "arbitrary"), vmem_limit_bytes=V7X_VMEM_LIMIT),
        name=name,
    )(*operands)


def _post_kernel(x_ref, yst_ref, yft_ref, gate_ref, mod_ref, gffn_ref, gfin_ref, ws_ref, wfx_ref,
                 wo_ref, wg_ref, wu_ref, wd_ref, o_ref, *, bounds):
    d = x_ref.shape[-1]
    mod = mod_ref[...]
    a = jnp.dot(yst_ref[...].T, ws_ref[...], preferred_element_type=F32)
    b = jnp.dot(yft_ref[...].T, wfx_ref[...], preferred_element_type=F32)
    gate = gate_ref[...].astype(F32)
    merged = gate[:, :d] * a + gate[:, d:] * b
    x1 = x_ref[...] + mod[2:3, :] * jnp.dot(merged.astype(BF16), wo_ref[...],
                                            preferred_element_type=F32)
    h2 = _rms_modulate(x1, gffn_ref[...], mod[4:5, :], mod[3:4, :]).astype(BF16)
    ffn = None
    for lo, hi in zip(bounds[:-1], bounds[1:]):
        sl = slice(lo, hi)
        gt = jnp.dot(h2, wg_ref[:, sl], preferred_element_type=F32)
        up = jnp.dot(h2, wu_ref[:, sl], preferred_element_type=F32)
        act = (gt * jax.nn.sigmoid(gt) * up).astype(BF16)
        part = jnp.dot(act, wd_ref[sl, :], preferred_element_type=F32)
        ffn = part if ffn is None else ffn + part
    x2 = x1 + mod[5:6, :] * ffn
    ms = jnp.mean(x2 * x2, axis=-1, keepdims=True)
    o_ref[...] = x2 * lax.rsqrt(ms + RMS_EPS) * gfin_ref[...]


def _post_call(x, yt_sb, yt_fx, gates, mod3, g_ffn, g_final, w_bs, w_bf, w_out, w_gate, w_up,
               w_down, tm):
    b, s, d = x.shape
    row = lambda bi, i: (bi, i, 0)
    yt_spec = pl.BlockSpec((None, D_HEADS, tm), lambda bi, i: (bi, 0, i))
    dff = w_gate.shape[1]
    half = -(-dff // (2 * V7X_MXU_WIDTH)) * V7X_MXU_WIDTH
    bounds = (0, half, dff) if half < dff else (0, dff)
    weights = (w_bs, w_bf, w_out, w_gate, w_up, w_down)
    return pl.pallas_call(
        functools.partial(_post_kernel, bounds=bounds),
        grid=(b, s // tm),
        in_specs=[pl.BlockSpec((None, tm, d), row), yt_spec, yt_spec,
                  pl.BlockSpec((None, tm, 2 * d), row),
                  pl.BlockSpec((None, N_MOD, d), lambda bi, i: (bi, 0, 0)),
                  _const_spec((1, d)), _const_spec((1, d))]
        + [_const_spec(w.shape) for w in weights],
        out_specs=pl.BlockSpec((None, tm, d), row),
        out_shape=jax.ShapeDtypeStruct((b, s, d), F32),
        compiler_params=pltpu.CompilerParams(
            dimension_semantics=("parallel", "parallel"), vmem_limit_bytes=V7X_VMEM_LIMIT),
        name="post",
    )(x, yt_sb, yt_fx, gates, mod3, g_ffn, g_final, *weights)


def kernel(x, c, w_ada, b_ada, g_mix, w_in, b_forget, b_gate, w_branch_sb, w_branch_fox,
           w_out, g_ffn, w_ffn_gate, w_ffn_up, w_ffn_down, g_final):
    b, s, d = x.shape
    depth = w_ada.shape[0]
    assert depth == 1, "the final RMSNorm is fused into the last (only) layer's closing call"
    tm = min(512, s)
    tq = min(256, s)
    dh = D_HEADS
    for l in range(depth):
        mod3 = _mod_call(c, w_ada[l], b_ada[l]).reshape(b, N_MOD, d)
        w = w_in[l]
        w_k = jnp.concatenate([w[:, dh:2 * dh], w[:, 4 * dh:5 * dh]], axis=1).astype(BF16)
        w_qvt = jnp.concatenate([w[:, 0:dh], w[:, 3 * dh:4 * dh], w[:, 2 * dh:3 * dh],
                                 w[:, 5 * dh:6 * dh], w[:, 6 * dh:6 * dh + N_HEADS],
                                 jnp.zeros((d, F_PAD_ROWS - N_HEADS), w.dtype)],
                                axis=1).T.astype(BF16)
        w_g = w[:, 6 * dh + N_HEADS:].astype(BF16)
        k_sb, k_fx, qt_sb, qt_fx, vt_sb, vt_fx, lf_t, gates = _inproj_call(
            x, mod3, g_mix[l].reshape(1, d), w_k, w_qvt, w_g, b_forget[l].reshape(N_HEADS, 1),
            b_gate[l].reshape(1, 2 * d), tm, tq)

        yt_sb = _attn_call(qt_sb, k_sb, vt_sb, tq)
        aug = _cumsum_call(lf_t)
        yt_fx = _attn_call(qt_fx, k_fx, vt_fx, tq, aug)

        x = _post_call(x, yt_sb, yt_fx, gates, mod3, g_ffn[l].reshape(1, d), g_final.reshape(1, d),
                       w_branch_sb[l].astype(BF16), w_branch_fox[l].astype(BF16),
                       w_out[l].astype(BF16), w_ffn_gate[l].astype(BF16),
                       w_ffn_up[l].astype(BF16), w_ffn_down[l].astype(BF16), tm)
    return x
```

```python
import functools
import math

import jax
import jax.numpy as jnp
from jax import lax
from jax.experimental import pallas as pl
from jax.experimental.pallas import tpu as pltpu

F32 = jnp.float32
BF16 = jnp.bfloat16

HEAD_DIM = 64
N_HEADS = 8
D_HEADS = N_HEADS * HEAD_DIM
PAIR = 2 * HEAD_DIM
N_MOD = 6
RMS_EPS = 1e-6
LOG2E = math.log2(math.e)
SB_Q_SCALE = HEAD_DIM ** -0.5
FOX_Q_SCALE = LOG2E * HEAD_DIM ** -0.5
AUG = 16
DEN_ROWS = 16
F_PAD_ROWS = 16
NEG_BIG = -1e30
SB_DEAD = 160.0 * math.log(2.0)
V7X_VMEM_LIMIT = 56 * 1024 * 1024
V7X_MXU_WIDTH = 256
NT_DIMS = (((1,), (1,)), ((), ()))


def _rms_modulate(x, g, scale, shift):
    ms = jnp.mean(x * x, axis=-1, keepdims=True)
    y = x * lax.rsqrt(ms + RMS_EPS) * g
    return y * (1.0 + scale) + shift


def _const_spec(shape):
    return pl.BlockSpec(shape, lambda *_: (0,) * len(shape), pipeline_mode=pl.Buffered(1))


def _mod_kernel(c_ref, w_ref, b_ref, o_ref):
    c = c_ref[...]
    ca = c * jax.nn.sigmoid(c)
    o_ref[...] = jnp.dot(ca, w_ref[...], preferred_element_type=F32,
                         precision=lax.Precision.HIGHEST) + b_ref[...]


def _mod_call(c, w_ada, b_ada, layer):
    b, d = c.shape
    depth, _, n = w_ada.shape
    tn = 1024
    return pl.pallas_call(
        _mod_kernel,
        grid=(n // tn,),
        in_specs=[pl.BlockSpec((b, d), lambda j: (0, 0)),
                  pl.BlockSpec((None, d, tn), lambda j: (layer, 0, j)),
                  pl.BlockSpec((None, 1, tn), lambda j: (layer, 0, j))],
        out_specs=pl.BlockSpec((b, tn), lambda j: (0, j)),
        out_shape=jax.ShapeDtypeStruct((b, n), F32),
        name="mod",
    )(c, w_ada, b_ada.reshape(depth, 1, n))


def _inproj_kernel(x_ref, mod_ref, g_ref, wk_ref, wqvt_ref, wg_ref, bf_ref, bg_ref,
                   ks_ref, kf_ref, qts_ref, qtf_ref, vts_ref, vtf_ref, lft_ref, gate_ref, *, tk):
    mod = mod_ref[...]
    h = _rms_modulate(x_ref[...], g_ref[...], mod[1:2, :], mod[0:1, :]).astype(BF16)
    d = D_HEADS
    kk = jnp.dot(h, wk_ref[...], preferred_element_type=F32)
    ks_ref[...] = kk[:, :d].astype(BF16)
    kf_ref[...] = kk[:, d:].astype(BF16)
    t = lax.dot_general(wqvt_ref[...], h, NT_DIMS, preferred_element_type=F32)
    qts_ref[...] = (t[0 * d:1 * d] * SB_Q_SCALE).astype(BF16)
    qtf_ref[...] = (t[1 * d:2 * d] * FOX_Q_SCALE).astype(BF16)
    for c in range(vts_ref.shape[0]):
        vts_ref[c] = t[2 * d:3 * d, c * tk:(c + 1) * tk].astype(BF16)
        vtf_ref[c] = t[3 * d:4 * d, c * tk:(c + 1) * tk].astype(BF16)
    f = t[4 * d:4 * d + N_HEADS] + bf_ref[...]
    lft_ref[...] = -(jnp.maximum(-f, 0.0) + jnp.log(1.0 + jnp.exp(-jnp.abs(f)))) * LOG2E
    gl = jnp.dot(h, wg_ref[...], preferred_element_type=F32) + bg_ref[...]
    gate_ref[...] = jax.nn.sigmoid(gl).astype(BF16)


def _inproj_call(x, mod3, g_mix, w_k, w_qvt, w_g, b_f, b_g, tm, tk):
    b, s, d = x.shape
    row = lambda bi, i: (bi, i, 0)
    col = lambda bi, i: (bi, 0, i)
    k_spec = pl.BlockSpec((None, tm, D_HEADS), row)
    qt_spec = pl.BlockSpec((None, D_HEADS, tm), col)
    vt_spec = pl.BlockSpec((None, tm // tk, D_HEADS, tk), lambda bi, i: (bi, i, 0, 0))
    return pl.pallas_call(
        functools.partial(_inproj_kernel, tk=tk),
        grid=(b, s // tm),
        in_specs=[pl.BlockSpec((None, tm, d), row),
                  pl.BlockSpec((None, N_MOD, d), lambda bi, i: (bi, 0, 0)),
                  _const_spec((1, d)),
                  _const_spec(w_k.shape), _const_spec(w_qvt.shape),
                  _const_spec(w_g.shape), _const_spec(b_f.shape), _const_spec(b_g.shape)],
        out_specs=[k_spec, k_spec, qt_spec, qt_spec, vt_spec, vt_spec,
                   pl.BlockSpec((None, N_HEADS, tm), col),
                   pl.BlockSpec((None, tm, 2 * d), row)],
        out_shape=[jax.ShapeDtypeStruct((b, s, D_HEADS), BF16)] * 2
        + [jax.ShapeDtypeStruct((b, D_HEADS, s), BF16)] * 2
        + [jax.ShapeDtypeStruct((b, s // tk, D_HEADS, tk), BF16)] * 2
        + [jax.ShapeDtypeStruct((b, N_HEADS, s), F32),
           jax.ShapeDtypeStruct((b, s, 2 * d), BF16)],
        compiler_params=pltpu.CompilerParams(
            dimension_semantics=("parallel", "parallel"), vmem_limit_bytes=V7X_VMEM_LIMIT),
        name="inproj",
    )(x, mod3, g_mix, w_k, w_qvt, w_g, b_f, b_g)


def _split3(v):
    hi = v.astype(BF16).astype(F32)
    r = v - hi
    mid = r.astype(BF16).astype(F32)
    lo = (r - mid).astype(BF16).astype(F32)
    return hi, mid, lo


def _cumsum_kernel(lft_ref, augk_ref, augqt_ref, *, chunk):
    h, s = lft_ref.shape
    n3 = 3 * h
    r = lax.broadcasted_iota(jnp.int32, (chunk, chunk), 0)
    c = lax.broadcasted_iota(jnp.int32, (chunk, chunk), 1)
    upper = (r <= c).astype(BF16)

    qr = lax.broadcasted_iota(jnp.int32, (h * AUG, n3), 0)
    qc = lax.broadcasted_iota(jnp.int32, (h * AUG, n3), 1)
    qrow = lax.broadcasted_iota(jnp.int32, (h * AUG, 1), 0)
    place_q = jnp.zeros((h * AUG, n3), F32)
    place_k = jnp.zeros((h * AUG, n3), F32)
    ones_q = jnp.zeros((h * AUG, 1), F32)
    ones_k = jnp.zeros((h * AUG, 1), F32)
    for hh in range(h):
        for p in range(3):
            place_q = jnp.where((qr == hh * AUG + 3 + p) & (qc == p * h + hh), 1.0, place_q)
            place_k = jnp.where((qr == hh * AUG + p) & (qc == p * h + hh), -1.0, place_k)
            ones_q = jnp.where(qrow == hh * AUG + p, 1.0, ones_q)
            ones_k = jnp.where(qrow == hh * AUG + 3 + p, 1.0, ones_k)
    place_q = place_q.astype(BF16)
    place_k = place_k.astype(BF16)

    carry = jnp.zeros((h, 1), F32)
    for ci in range(s // chunk):
        sl = slice(ci * chunk, (ci + 1) * chunk)
        pieces = jnp.concatenate(_split3(lft_ref[:, sl]), axis=0).astype(BF16)
        p3 = jnp.dot(pieces, upper, preferred_element_type=F32)
        cum = (p3[0:h] + p3[h:2 * h]) + p3[2 * h:n3] + carry
        carry = cum[:, chunk - 1:chunk]
        stacked = jnp.concatenate(_split3(cum), axis=0).astype(BF16)
        augqt_ref[:, sl] = (jnp.dot(place_q, stacked, preferred_element_type=F32)
                            + ones_q).astype(BF16)
        augkt = jnp.dot(place_k, stacked, preferred_element_type=F32) + ones_k
        augk_ref[sl, :] = augkt.T.astype(BF16)


def _cumsum_call(lf_t):
    b, h, s = lf_t.shape
    return pl.pallas_call(
        functools.partial(_cumsum_kernel, chunk=min(256, s)),
        grid=(b,),
        in_specs=[pl.BlockSpec((None, h, s), lambda bi: (bi, 0, 0))],
        out_specs=[pl.BlockSpec((None, s, h * AUG), lambda bi: (bi, 0, 0)),
                   pl.BlockSpec((None, h * AUG, s), lambda bi: (bi, 0, 0))],
        out_shape=[jax.ShapeDtypeStruct((b, s, h * AUG), BF16),
                   jax.ShapeDtypeStruct((b, h * AUG, s), BF16)],
        compiler_params=pltpu.CompilerParams(dimension_semantics=("parallel",)),
        name="cumsum",
    )(lf_t)


def _softplus(z):
    return jnp.maximum(z, 0.0) + jnp.log(1.0 + jnp.exp(-jnp.abs(z)))


def _fill_query_operand(rhs_ref, qt_ref, augqt_ref, first_tile):
    @pl.when(first_tile)
    def _():
        rhs_ref[...] = jnp.zeros(rhs_ref.shape, rhs_ref.dtype)

    for hh in range(N_HEADS):
        half = hh % 2
        rhs_ref[hh, half * HEAD_DIM:(half + 1) * HEAD_DIM, :] = (
            qt_ref[hh * HEAD_DIM:(hh + 1) * HEAD_DIM, :])
        if augqt_ref is not None:
            rhs_ref[hh, PAIR + hh * AUG:PAIR + (hh + 1) * AUG, :] = (
                augqt_ref[hh * AUG:(hh + 1) * AUG, :])


def _sb_kernel(qt_ref, k_ref, vt_ref, o_ref, rhs_ref, z_ref, acc_ref, run_ref, *, tq):
    g = N_HEADS
    i = pl.program_id(1)
    _fill_query_operand(rhs_ref, qt_ref, None, i == 0)
    rows = lax.broadcasted_iota(jnp.int32, (tq, tq), 0)
    cols = lax.broadcasted_iota(jnp.int32, (tq, tq), 1)
    suffix = (cols >= rows).astype(BF16)
    past = rows < cols

    def scores(j):
        start = pl.multiple_of(j * tq, tq)
        return [jnp.dot(k_ref[pl.ds(start, tq), (hh // 2) * PAIR:(hh // 2 + 1) * PAIR], rhs_ref[hh],
                        preferred_element_type=F32) for hh in range(g)]

    def block(j, masked):
        zs_next = scores(jnp.maximum(j - 1, 0))
        zs, cs, runs = [], [], []
        for hh in range(g):
            z = z_ref[hh]
            z_ref[hh] = zs_next[hh]
            zm = jnp.where(past, z, NEG_BIG) if masked else z
            sp = _softplus(zm.astype(BF16))
            cs.append(jnp.dot(suffix, sp, preferred_element_type=F32))
            zs.append(z)
        for hh in range(g):
            run = run_ref[hh:hh + 1, :]
            runs.append(run)
            run_ref[hh:hh + 1, :] = run + cs[hh][0:1, :]
        lowest = jnp.min(run_ref[...])
        for hh in range(g):
            w = jnp.exp(zs[hh] - cs[hh] - runs[hh])
            if masked:
                w = jnp.where(past, w, 0.0)
            vt = vt_ref[j, hh * HEAD_DIM:(hh + 1) * HEAD_DIM, :]
            acc_ref[hh] += jnp.dot(vt, w.astype(BF16), preferred_element_type=F32)
        return lowest

    for hh, z in enumerate(scores(i)):
        z_ref[hh] = z
    acc_ref[...] = jnp.zeros(acc_ref.shape, F32)
    run_ref[...] = jnp.zeros(run_ref.shape, F32)
    lowest = block(i, True)

    def cond(state):
        n, lowest = state
        return jnp.logical_and(n < i, lowest < SB_DEAD)

    def body(state):
        n, _ = state
        return n + 1, block(i - 1 - n, False)

    lax.while_loop(cond, body, (jnp.int32(0), lowest))
    for hh in range(g):
        o_ref[hh * HEAD_DIM:(hh + 1) * HEAD_DIM, :] = acc_ref[hh].astype(o_ref.dtype)


def _fox_kernel(qt_ref, augqt_ref, k_ref, augk_ref, vt_ref, o_ref, rhs_ref, s_ref, acc_ref, m_ref,
                bmax_ref, *, tq):
    g = N_HEADS
    i = pl.program_id(1)
    _fill_query_operand(rhs_ref, qt_ref, augqt_ref, i == 0)
    den_rows = jnp.ones((DEN_ROWS, tq), BF16)

    def scores(hh, j, diagonal):
        start = pl.multiple_of(j * tq, tq)
        pair = slice((hh // 2) * PAIR, (hh // 2 + 1) * PAIR)
        lhs = jnp.concatenate([k_ref[pl.ds(start, tq), pair], augk_ref[pl.ds(start, tq), :]],
                              axis=1)
        s = jnp.dot(lhs, rhs_ref[hh], preferred_element_type=F32)
        if diagonal:
            rows = lax.broadcasted_iota(jnp.int32, (tq, tq), 0)
            cols = lax.broadcasted_iota(jnp.int32, (tq, tq), 1)
            s = jnp.where(rows <= cols, s, NEG_BIG)
        return s

    def keep(hh, s):
        s_ref[hh] = s
        bmax_ref[hh:hh + 1, :] = jnp.max(s, axis=0, keepdims=True)

    def block(j, j_next):
        for hh in range(g):
            s_next = scores(hh, j_next, False)
            m = m_ref[hh:hh + 1, :]
            m_new = jnp.maximum(m, bmax_ref[hh:hh + 1, :])
            alpha = jnp.exp2(m - m_new)
            p = jnp.exp2(s_ref[hh] - m_new).astype(BF16)
            m_ref[hh:hh + 1, :] = m_new
            vt = jnp.concatenate([vt_ref[j, hh * HEAD_DIM:(hh + 1) * HEAD_DIM, :], den_rows], axis=0)
            acc_ref[hh] = alpha * acc_ref[hh] + jnp.dot(vt, p, preferred_element_type=F32)
            keep(hh, s_next)

    for hh in range(g):
        keep(hh, scores(hh, i, True))
    acc_ref[...] = jnp.zeros(acc_ref.shape, F32)
    m_ref[...] = jnp.full(m_ref.shape, NEG_BIG, F32)
    block(i, 0)
    last = jnp.maximum(i - 1, 0)

    @pl.loop(0, i // 2)
    def _(n):
        block(2 * n, 2 * n + 1)
        block(2 * n + 1, jnp.minimum(2 * n + 2, last))

    @pl.when(i % 2 == 1)
    def _():
        block(last, last)

    for hh in range(g):
        o_ref[hh * HEAD_DIM:(hh + 1) * HEAD_DIM, :] = (
            acc_ref[hh, :HEAD_DIM, :] / acc_ref[hh, HEAD_DIM:HEAD_DIM + 1, :]).astype(o_ref.dtype)


def _attn_call(qt, k, vt, tq, aug=None):
    b, d, s = qt.shape
    nk = s // tq
    q_spec = pl.BlockSpec((None, d, tq), lambda bi, i: (bi, 0, i))
    k_spec = pl.BlockSpec((None, s, d), lambda bi, i: (bi, 0, 0))
    vt_spec = pl.BlockSpec((None, nk, d, tq), lambda bi, i: (bi, 0, 0, 0))
    if aug is None:
        body, name, kdim, acc_rows, n_row_states = _sb_kernel, "sb_attn", PAIR, HEAD_DIM, 1
        operands, in_specs = (qt, k, vt), [q_spec, k_spec, vt_spec]
    else:
        augk, augqt = aug
        body, name, kdim, acc_rows = _fox_kernel, "fox_attn", 2 * PAIR, HEAD_DIM + DEN_ROWS
        n_row_states = 2
        operands = (qt, augqt, k, augk, vt)
        in_specs = [q_spec, pl.BlockSpec((None, N_HEADS * AUG, tq), lambda bi, i: (bi, 0, i)),
                    k_spec, pl.BlockSpec((None, s, N_HEADS * AUG), lambda bi, i: (bi, 0, 0)),
                    vt_spec]
    return pl.pallas_call(
        functools.partial(body, tq=tq),
        grid=(b, s // tq),
        in_specs=in_specs,
        out_specs=q_spec,
        out_shape=jax.ShapeDtypeStruct((b, d, s), BF16),
        scratch_shapes=[pltpu.VMEM((N_HEADS, kdim, tq), BF16),
                        pltpu.VMEM((N_HEADS, tq, tq), F32),
                        pltpu.VMEM((N_HEADS, acc_rows, tq), F32)]
        + [pltpu.VMEM((N_HEADS, tq), F32)] * n_row_states,
        compiler_params=pltpu.CompilerParams(
            dimension_semantics=("parallel", "arbitrary"), vmem_limit_bytes=V7X_VMEM_LIMIT),
        name=name,
    )(*operands)


def _post_kernel(x_ref, yst_ref, yft_ref, gate_ref, mod_ref, gffn_ref, gfin_ref, ws_ref, wfx_ref,
                 wo_ref, wg_ref, wu_ref, wd_ref, o_ref, *, bounds):
    d = x_ref.shape[-1]
    mod = mod_ref[...]
    a = jnp.dot(yst_ref[...].T, ws_ref[...], preferred_element_type=F32)
    b = jnp.dot(yft_ref[...].T, wfx_ref[...], preferred_element_type=F32)
    gate = gate_ref[...].astype(F32)
    merged = gate[:, :d] * a + gate[:, d:] * b
    x1 = x_ref[...] + mod[2:3, :] * jnp.dot(merged.astype(BF16), wo_ref[...],
                                            preferred_element_type=F32)
    h2 = _rms_modulate(x1, gffn_ref[...], mod[4:5, :], mod[3:4, :]).astype(BF16)
    ffn = None
    for lo, hi in zip(bounds[:-1], bounds[1:]):
        sl = slice(lo, hi)
        gt = jnp.dot(h2, wg_ref[:, sl], preferred_element_type=F32)
        up = jnp.dot(h2, wu_ref[:, sl], preferred_element_type=F32)
        act = (gt * jax.nn.sigmoid(gt) * up).astype(BF16)
        part = jnp.dot(act, wd_ref[sl, :], preferred_element_type=F32)
        ffn = part if ffn is None else ffn + part
    x2 = x1 + mod[5:6, :] * ffn
    ms = jnp.mean(x2 * x2, axis=-1, keepdims=True)
    o_ref[...] = x2 * lax.rsqrt(ms + RMS_EPS) * gfin_ref[...]


def _post_call(x, yt_sb, yt_fx, gates, mod3, g_ffn, g_final, w_bs, w_bf, w_out, w_gate, w_up,
               w_down, tm):
    b, s, d = x.shape
    row = lambda bi, i: (bi, i, 0)
    yt_spec = pl.BlockSpec((None, D_HEADS, tm), lambda bi, i: (bi, 0, i))
    dff = w_gate.shape[1]
    half = -(-dff // (2 * V7X_MXU_WIDTH)) * V7X_MXU_WIDTH
    bounds = (0, half, dff) if half < dff else (0, dff)
    weights = (w_bs, w_bf, w_out, w_gate, w_up, w_down)
    return pl.pallas_call(
        functools.partial(_post_kernel, bounds=bounds),
        grid=(b, s // tm),
        in_specs=[pl.BlockSpec((None, tm, d), row), yt_spec, yt_spec,
                  pl.BlockSpec((None, tm, 2 * d), row),
                  pl.BlockSpec((None, N_MOD, d), lambda bi, i: (bi, 0, 0)),
                  _const_spec((1, d)), _const_spec((1, d))]
        + [_const_spec(w.shape) for w in weights],
        out_specs=pl.BlockSpec((None, tm, d), row),
        out_shape=jax.ShapeDtypeStruct((b, s, d), F32),
        compiler_params=pltpu.CompilerParams(
            dimension_semantics=("parallel", "parallel"), vmem_limit_bytes=V7X_VMEM_LIMIT),
        name="post",
    )(x, yt_sb, yt_fx, gates, mod3, g_ffn, g_final, *weights)


def kernel(x, c, w_ada, b_ada, g_mix, w_in, b_forget, b_gate, w_branch_sb, w_branch_fox,
           w_out, g_ffn, w_ffn_gate, w_ffn_up, w_ffn_down, g_final):
    b, s, d = x.shape
    depth = w_ada.shape[0]
    assert depth == 1, "the final RMSNorm is fused into the last (only) layer's closing call"
    tm = min(512, s)
    tq = min(256, s)
    dh = D_HEADS
    for l in range(depth):
        mod3 = _mod_call(c, w_ada, b_ada, l).reshape(b, N_MOD, d)
        w = w_in[l]
        w_k = jnp.concatenate([w[:, dh:2 * dh], w[:, 4 * dh:5 * dh]], axis=1).astype(BF16)
        w_qvt = jnp.concatenate([w[:, 0:dh], w[:, 3 * dh:4 * dh], w[:, 2 * dh:3 * dh],
                                 w[:, 5 * dh:6 * dh], w[:, 6 * dh:6 * dh + N_HEADS],
                                 jnp.zeros((d, F_PAD_ROWS - N_HEADS), w.dtype)],
                                axis=1).T.astype(BF16)
        w_g = w[:, 6 * dh + N_HEADS:].astype(BF16)
        k_sb, k_fx, qt_sb, qt_fx, vt_sb, vt_fx, lf_t, gates = _inproj_call(
            x, mod3, g_mix[l].reshape(1, d), w_k, w_qvt, w_g, b_forget[l].reshape(N_HEADS, 1),
            b_gate[l].reshape(1, 2 * d), tm, tq)

        yt_sb = _attn_call(qt_sb, k_sb, vt_sb, tq)
        aug = _cumsum_call(lf_t)
        yt_fx = _attn_call(qt_fx, k_fx, vt_fx, tq, aug)

        x = _post_call(x, yt_sb, yt_fx, gates, mod3, g_ffn[l].reshape(1, d), g_final.reshape(1, d),
                       w_branch_sb[l].astype(BF16), w_branch_fox[l].astype(BF16),
                       w_out[l].astype(BF16), w_ffn_gate[l].astype(BF16),
                       w_ffn_up[l].astype(BF16), w_ffn_down[l].astype(BF16), tm)
    return x
```

```python
import functools
import math

import jax
import jax.numpy as jnp
from jax import lax
from jax.experimental import pallas as pl
from jax.experimental.pallas import tpu as pltpu

F32 = jnp.float32
BF16 = jnp.bfloat16

HEAD_DIM = 64
N_HEADS = 8
D_HEADS = N_HEADS * HEAD_DIM
PAIR = 2 * HEAD_DIM
N_MOD = 6
RMS_EPS = 1e-6
LOG2E = math.log2(math.e)
SB_Q_SCALE = HEAD_DIM ** -0.5
FOX_Q_SCALE = LOG2E * HEAD_DIM ** -0.5
AUG = 16
DEN_ROWS = 16
PREFIX_LAG = 2
F_PAD_ROWS = 16
NEG_BIG = -1e30
SB_DEAD = 160.0 * math.log(2.0)
V7X_VMEM_LIMIT = 56 * 1024 * 1024
V7X_MXU_WIDTH = 256
NT_DIMS = (((1,), (1,)), ((), ()))


def _rms_modulate(x, g, scale, shift):
    ms = jnp.mean(x * x, axis=-1, keepdims=True)
    y = x * lax.rsqrt(ms + RMS_EPS) * g
    return y * (1.0 + scale) + shift


def _const_spec(shape):
    return pl.BlockSpec(shape, lambda *_: (0,) * len(shape), pipeline_mode=pl.Buffered(1))


def _mod_kernel(c_ref, w_ref, b_ref, o_ref):
    c = c_ref[...]
    ca = c * jax.nn.sigmoid(c)
    o_ref[...] = jnp.dot(ca, w_ref[...], preferred_element_type=F32,
                         precision=lax.Precision.HIGHEST) + b_ref[...]


def _mod_call(c, w_ada, b_ada, layer):
    b, d = c.shape
    depth, _, n = w_ada.shape
    tn = 1024
    return pl.pallas_call(
        _mod_kernel,
        grid=(n // tn,),
        in_specs=[pl.BlockSpec((b, d), lambda j: (0, 0)),
                  pl.BlockSpec((None, d, tn), lambda j: (layer, 0, j)),
                  pl.BlockSpec((None, 1, tn), lambda j: (layer, 0, j))],
        out_specs=pl.BlockSpec((b, tn), lambda j: (0, j)),
        out_shape=jax.ShapeDtypeStruct((b, n), F32),
        name="mod",
    )(c, w_ada, b_ada.reshape(depth, 1, n))


def _inproj_kernel(x_ref, mod_ref, g_ref, wk_ref, wqvt_ref, wg_ref, bf_ref, bg_ref,
                   ks_ref, kf_ref, qts_ref, qtf_ref, vts_ref, vtf_ref, lft_ref, gate_ref, *, tk):
    mod = mod_ref[...]
    h = _rms_modulate(x_ref[...], g_ref[...], mod[1:2, :], mod[0:1, :]).astype(BF16)
    d = D_HEADS
    kk = jnp.dot(h, wk_ref[...], preferred_element_type=F32)
    ks_ref[...] = kk[:, :d].astype(BF16)
    kf_ref[...] = kk[:, d:].astype(BF16)
    t = lax.dot_general(wqvt_ref[...], h, NT_DIMS, preferred_element_type=F32)
    qts_ref[...] = (t[0 * d:1 * d] * SB_Q_SCALE).astype(BF16)
    qtf_ref[...] = (t[1 * d:2 * d] * FOX_Q_SCALE).astype(BF16)
    for c in range(vts_ref.shape[0]):
        vts_ref[c] = t[2 * d:3 * d, c * tk:(c + 1) * tk].astype(BF16)
        vtf_ref[c] = t[3 * d:4 * d, c * tk:(c + 1) * tk].astype(BF16)
    f = t[4 * d:4 * d + N_HEADS] + bf_ref[...]
    lft_ref[...] = -(jnp.maximum(-f, 0.0) + jnp.log(1.0 + jnp.exp(-jnp.abs(f)))) * LOG2E
    gl = jnp.dot(h, wg_ref[...], preferred_element_type=F32) + bg_ref[...]
    gate_ref[...] = jax.nn.sigmoid(gl).astype(BF16)


def _inproj_call(x, mod3, g_mix, w_k, w_qvt, w_g, b_f, b_g, tm, tk):
    b, s, d = x.shape
    row = lambda bi, i: (bi, i, 0)
    col = lambda bi, i: (bi, 0, i)
    k_spec = pl.BlockSpec((None, tm, D_HEADS), row)
    qt_spec = pl.BlockSpec((None, D_HEADS, tm), col)
    vt_spec = pl.BlockSpec((None, tm // tk, D_HEADS, tk), lambda bi, i: (bi, i, 0, 0))
    return pl.pallas_call(
        functools.partial(_inproj_kernel, tk=tk),
        grid=(b, s // tm),
        in_specs=[pl.BlockSpec((None, tm, d), row),
                  pl.BlockSpec((None, N_MOD, d), lambda bi, i: (bi, 0, 0)),
                  _const_spec((1, d)),
                  _const_spec(w_k.shape), _const_spec(w_qvt.shape),
                  _const_spec(w_g.shape), _const_spec(b_f.shape), _const_spec(b_g.shape)],
        out_specs=[k_spec, k_spec, qt_spec, qt_spec, vt_spec, vt_spec,
                   pl.BlockSpec((None, N_HEADS, tm), col),
                   pl.BlockSpec((None, tm, 2 * d), row)],
        out_shape=[jax.ShapeDtypeStruct((b, s, D_HEADS), BF16)] * 2
        + [jax.ShapeDtypeStruct((b, D_HEADS, s), BF16)] * 2
        + [jax.ShapeDtypeStruct((b, s // tk, D_HEADS, tk), BF16)] * 2
        + [jax.ShapeDtypeStruct((b, N_HEADS, s), F32),
           jax.ShapeDtypeStruct((b, s, 2 * d), BF16)],
        compiler_params=pltpu.CompilerParams(
            dimension_semantics=("parallel", "parallel"), vmem_limit_bytes=V7X_VMEM_LIMIT),
        name="inproj",
    )(x, mod3, g_mix, w_k, w_qvt, w_g, b_f, b_g)


def _split3(v):
    hi = v.astype(BF16).astype(F32)
    r = v - hi
    mid = r.astype(BF16).astype(F32)
    lo = (r - mid).astype(BF16).astype(F32)
    return hi, mid, lo


def _cumsum_kernel(lft_ref, augk_ref, augqt_ref, *, chunk):
    h, s = lft_ref.shape
    n3 = 3 * h
    r = lax.broadcasted_iota(jnp.int32, (chunk, chunk), 0)
    c = lax.broadcasted_iota(jnp.int32, (chunk, chunk), 1)
    upper = (r <= c).astype(BF16)

    qr = lax.broadcasted_iota(jnp.int32, (h * AUG, n3), 0)
    qc = lax.broadcasted_iota(jnp.int32, (h * AUG, n3), 1)
    qrow = lax.broadcasted_iota(jnp.int32, (h * AUG, 1), 0)
    place_q = jnp.zeros((h * AUG, n3), F32)
    place_k = jnp.zeros((h * AUG, n3), F32)
    ones_q = jnp.zeros((h * AUG, 1), F32)
    ones_k = jnp.zeros((h * AUG, 1), F32)
    for hh in range(h):
        for p in range(3):
            place_q = jnp.where((qr == hh * AUG + 3 + p) & (qc == p * h + hh), 1.0, place_q)
            place_k = jnp.where((qr == hh * AUG + p) & (qc == p * h + hh), -1.0, place_k)
            ones_q = jnp.where(qrow == hh * AUG + p, 1.0, ones_q)
            ones_k = jnp.where(qrow == hh * AUG + 3 + p, 1.0, ones_k)
    place_q = place_q.astype(BF16)
    place_k = place_k.astype(BF16)

    carry = jnp.zeros((h, 1), F32)
    for ci in range(s // chunk):
        sl = slice(ci * chunk, (ci + 1) * chunk)
        pieces = jnp.concatenate(_split3(lft_ref[:, sl]), axis=0).astype(BF16)
        p3 = jnp.dot(pieces, upper, preferred_element_type=F32)
        cum = (p3[0:h] + p3[h:2 * h]) + p3[2 * h:n3] + carry
        carry = cum[:, chunk - 1:chunk]
        stacked = jnp.concatenate(_split3(cum), axis=0).astype(BF16)
        augqt_ref[:, sl] = (jnp.dot(place_q, stacked, preferred_element_type=F32)
                            + ones_q).astype(BF16)
        augkt = jnp.dot(place_k, stacked, preferred_element_type=F32) + ones_k
        augk_ref[sl, :] = augkt.T.astype(BF16)


def _cumsum_call(lf_t):
    b, h, s = lf_t.shape
    return pl.pallas_call(
        functools.partial(_cumsum_kernel, chunk=min(256, s)),
        grid=(b,),
        in_specs=[pl.BlockSpec((None, h, s), lambda bi: (bi, 0, 0))],
        out_specs=[pl.BlockSpec((None, s, h * AUG), lambda bi: (bi, 0, 0)),
                   pl.BlockSpec((None, h * AUG, s), lambda bi: (bi, 0, 0))],
        out_shape=[jax.ShapeDtypeStruct((b, s, h * AUG), BF16),
                   jax.ShapeDtypeStruct((b, h * AUG, s), BF16)],
        compiler_params=pltpu.CompilerParams(dimension_semantics=("parallel",)),
        name="cumsum",
    )(lf_t)


def _softplus(z):
    return jnp.maximum(z, 0.0) + jnp.log(1.0 + jnp.exp(-jnp.abs(z)))


def _fill_query_operand(rhs_ref, qt_ref, augqt_ref, first_tile):
    @pl.when(first_tile)
    def _():
        rhs_ref[...] = jnp.zeros(rhs_ref.shape, rhs_ref.dtype)

    for hh in range(N_HEADS):
        half = hh % 2
        rhs_ref[hh, half * HEAD_DIM:(half + 1) * HEAD_DIM, :] = (
            qt_ref[hh * HEAD_DIM:(hh + 1) * HEAD_DIM, :])
        if augqt_ref is not None:
            rhs_ref[hh, PAIR + hh * AUG:PAIR + (hh + 1) * AUG, :] = (
                augqt_ref[hh * AUG:(hh + 1) * AUG, :])


def _sb_kernel(qt_ref, k_ref, vt_ref, o_ref, rhs_ref, z_ref, acc_ref, run_ref, c_ref, *, tq):
    g = N_HEADS
    i = pl.program_id(1)
    _fill_query_operand(rhs_ref, qt_ref, None, i == 0)
    rows = lax.broadcasted_iota(jnp.int32, (tq, tq), 0)
    cols = lax.broadcasted_iota(jnp.int32, (tq, tq), 1)
    suffix = (cols >= rows).astype(BF16)

    def scores(hh, j, diagonal):
        start = pl.multiple_of(j * tq, tq)
        z = jnp.dot(k_ref[pl.ds(start, tq), (hh // 2) * PAIR:(hh // 2 + 1) * PAIR], rhs_ref[hh],
                    preferred_element_type=F32)
        return jnp.where(rows < cols, z, NEG_BIG) if diagonal else z

    def prefix_sums(hh):
        sp = _softplus(z_ref[hh].astype(BF16))
        c_ref[hh] = jnp.dot(suffix, sp, preferred_element_type=F32)

    def trip(j, j_next):
        for t in range(g + PREFIX_LAG):
            if t < g:
                z_next = scores(t, j_next, False)
                c = c_ref[t]
                run = run_ref[t:t + 1, :]
                w = jnp.exp(z_ref[t] - c - run)
                acc_ref[t] += jnp.dot(vt_ref[j, t * HEAD_DIM:(t + 1) * HEAD_DIM, :], w.astype(BF16),
                                      preferred_element_type=F32)
                run_ref[t:t + 1, :] = run + c[0:1, :]
                z_ref[t] = z_next
            if t >= PREFIX_LAG:
                prefix_sums(t - PREFIX_LAG)
        return jnp.min(run_ref[...])

    for hh in range(g):
        z_ref[hh] = scores(hh, i, True)
    for hh in range(g):
        prefix_sums(hh)
    acc_ref[...] = jnp.zeros(acc_ref.shape, F32)
    run_ref[...] = jnp.zeros(run_ref.shape, F32)
    lowest = trip(i, jnp.maximum(i - 1, 0))

    def cond(state):
        n, lowest = state
        return jnp.logical_and(n < i, lowest < SB_DEAD)

    def body(state):
        n, _ = state
        j = i - 1 - n
        return n + 1, trip(j, jnp.maximum(j - 1, 0))

    lax.while_loop(cond, body, (jnp.int32(0), lowest))
    for hh in range(g):
        o_ref[hh * HEAD_DIM:(hh + 1) * HEAD_DIM, :] = acc_ref[hh].astype(o_ref.dtype)


def _fox_kernel(qt_ref, augqt_ref, k_ref, augk_ref, vt_ref, o_ref, rhs_ref, s_ref, acc_ref, m_ref,
                bmax_ref, *, tq):
    g = N_HEADS
    i = pl.program_id(1)
    _fill_query_operand(rhs_ref, qt_ref, augqt_ref, i == 0)
    den_rows = jnp.ones((DEN_ROWS, tq), BF16)

    def scores(hh, j, diagonal):
        start = pl.multiple_of(j * tq, tq)
        pair = slice((hh // 2) * PAIR, (hh // 2 + 1) * PAIR)
        lhs = jnp.concatenate([k_ref[pl.ds(start, tq), pair], augk_ref[pl.ds(start, tq), :]],
                              axis=1)
        s = jnp.dot(lhs, rhs_ref[hh], preferred_element_type=F32)
        if diagonal:
            rows = lax.broadcasted_iota(jnp.int32, (tq, tq), 0)
            cols = lax.broadcasted_iota(jnp.int32, (tq, tq), 1)
            s = jnp.where(rows <= cols, s, NEG_BIG)
        return s

    def keep(hh, s):
        s_ref[hh] = s
        bmax_ref[hh:hh + 1, :] = jnp.max(s, axis=0, keepdims=True)

    def block(j, j_next):
        for hh in range(g):
            s_next = scores(hh, j_next, False)
            m = m_ref[hh:hh + 1, :]
            m_new = jnp.maximum(m, bmax_ref[hh:hh + 1, :])
            alpha = jnp.exp2(m - m_new)
            p = jnp.exp2(s_ref[hh] - m_new).astype(BF16)
            m_ref[hh:hh + 1, :] = m_new
            vt = jnp.concatenate([vt_ref[j, hh * HEAD_DIM:(hh + 1) * HEAD_DIM, :], den_rows], axis=0)
            acc_ref[hh] = alpha * acc_ref[hh] + jnp.dot(vt, p, preferred_element_type=F32)
            keep(hh, s_next)

    for hh in range(g):
        keep(hh, scores(hh, i, True))
    acc_ref[...] = jnp.zeros(acc_ref.shape, F32)
    m_ref[...] = jnp.full(m_ref.shape, NEG_BIG, F32)
    block(i, 0)
    last = jnp.maximum(i - 1, 0)

    @pl.loop(0, i // 2)
    def _(n):
        block(2 * n, 2 * n + 1)
        block(2 * n + 1, jnp.minimum(2 * n + 2, last))

    @pl.when(i % 2 == 1)
    def _():
        block(last, last)

    for hh in range(g):
        o_ref[hh * HEAD_DIM:(hh + 1) * HEAD_DIM, :] = (
            acc_ref[hh, :HEAD_DIM, :] / acc_ref[hh, HEAD_DIM:HEAD_DIM + 1, :]).astype(o_ref.dtype)


def _attn_call(qt, k, vt, tq, aug=None):
    b, d, s = qt.shape
    nk = s // tq
    q_spec = pl.BlockSpec((None, d, tq), lambda bi, i: (bi, 0, i))
    k_spec = pl.BlockSpec((None, s, d), lambda bi, i: (bi, 0, 0))
    vt_spec = pl.BlockSpec((None, nk, d, tq), lambda bi, i: (bi, 0, 0, 0))
    if aug is None:
        body, name, kdim, acc_rows, n_row_states = _sb_kernel, "sb_attn", PAIR, HEAD_DIM, 1
        extra = [pltpu.VMEM((N_HEADS, tq, tq), F32)]
        operands, in_specs = (qt, k, vt), [q_spec, k_spec, vt_spec]
    else:
        augk, augqt = aug
        body, name, kdim, acc_rows = _fox_kernel, "fox_attn", 2 * PAIR, HEAD_DIM + DEN_ROWS
        n_row_states, extra = 2, []
        operands = (qt, augqt, k, augk, vt)
        in_specs = [q_spec, pl.BlockSpec((None, N_HEADS * AUG, tq), lambda bi, i: (bi, 0, i)),
                    k_spec, pl.BlockSpec((None, s, N_HEADS * AUG), lambda bi, i: (bi, 0, 0)),
                    vt_spec]
    return pl.pallas_call(
        functools.partial(body, tq=tq),
        grid=(b, s // tq),
        in_specs=in_specs,
        out_specs=q_spec,
        out_shape=jax.ShapeDtypeStruct((b, d, s), BF16),
        scratch_shapes=[pltpu.VMEM((N_HEADS, kdim, tq), BF16),
                        pltpu.VMEM((N_HEADS, tq, tq), F32),
                        pltpu.VMEM((N_HEADS, acc_rows, tq), F32)]
        + [pltpu.VMEM((N_HEADS, tq), F32)] * n_row_states + extra,
        compiler_params=pltpu.CompilerParams(
            dimension_semantics=("parallel", "arbitrary"), vmem_limit_bytes=V7X_VMEM_LIMIT),
        name=name,
    )(*operands)


def _post_kernel(x_ref, yst_ref, yft_ref, gate_ref, mod_ref, gffn_ref, gfin_ref, ws_ref, wfx_ref,
                 wo_ref, wg_ref, wu_ref, wd_ref, o_ref, *, bounds):
    d = x_ref.shape[-1]
    mod = mod_ref[...]
    a = jnp.dot(yst_ref[...].T, ws_ref[...], preferred_element_type=F32)
    b = jnp.dot(yft_ref[...].T, wfx_ref[...], preferred_element_type=F32)
    gate = gate_ref[...].astype(F32)
    merged = gate[:, :d] * a + gate[:, d:] * b
    x1 = x_ref[...] + mod[2:3, :] * jnp.dot(merged.astype(BF16), wo_ref[...],
                                            preferred_element_type=F32)
    h2 = _rms_modulate(x1, gffn_ref[...], mod[4:5, :], mod[3:4, :]).astype(BF16)
    ffn = None
    for lo, hi in zip(bounds[:-1], bounds[1:]):
        sl = slice(lo, hi)
        gt = jnp.dot(h2, wg_ref[:, sl], preferred_element_type=F32)
        up = jnp.dot(h2, wu_ref[:, sl], preferred_element_type=F32)
        act = (gt * jax.nn.sigmoid(gt) * up).astype(BF16)
        part = jnp.dot(act, wd_ref[sl, :], preferred_element_type=F32)
        ffn = part if ffn is None else ffn + part
    x2 = x1 + mod[5:6, :] * ffn
    ms = jnp.mean(x2 * x2, axis=-1, keepdims=True)
    o_ref[...] = x2 * lax.rsqrt(ms + RMS_EPS) * gfin_ref[...]


def _post_call(x, yt_sb, yt_fx, gates, mod3, g_ffn, g_final, w_bs, w_bf, w_out, w_gate, w_up,
               w_down, tm):
    b, s, d = x.shape
    row = lambda bi, i: (bi, i, 0)
    yt_spec = pl.BlockSpec((None, D_HEADS, tm), lambda bi, i: (bi, 0, i))
    dff = w_gate.shape[1]
    half = -(-dff // (2 * V7X_MXU_WIDTH)) * V7X_MXU_WIDTH
    bounds = (0, half, dff) if half < dff else (0, dff)
    weights = (w_bs, w_bf, w_out, w_gate, w_up, w_down)
    return pl.pallas_call(
        functools.partial(_post_kernel, bounds=bounds),
        grid=(b, s // tm),
        in_specs=[pl.BlockSpec((None, tm, d), row), yt_spec, yt_spec,
                  pl.BlockSpec((None, tm, 2 * d), row),
                  pl.BlockSpec((None, N_MOD, d), lambda bi, i: (bi, 0, 0)),
                  _const_spec((1, d)), _const_spec((1, d))]
        + [_const_spec(w.shape) for w in weights],
        out_specs=pl.BlockSpec((None, tm, d), row),
        out_shape=jax.ShapeDtypeStruct((b, s, d), F32),
        compiler_params=pltpu.CompilerParams(
            dimension_semantics=("parallel", "parallel"), vmem_limit_bytes=V7X_VMEM_LIMIT),
        name="post",
    )(x, yt_sb, yt_fx, gates, mod3, g_ffn, g_final, *weights)


def kernel(x, c, w_ada, b_ada, g_mix, w_in, b_forget, b_gate, w_branch_sb, w_branch_fox,
           w_out, g_ffn, w_ffn_gate, w_ffn_up, w_ffn_down, g_final):
    b, s, d = x.shape
    depth = w_ada.shape[0]
    assert depth == 1, "the final RMSNorm is fused into the last (only) layer's closing call"
    tm = min(512, s)
    tq = min(256, s)
    dh = D_HEADS
    for l in range(depth):
        mod3 = _mod_call(c, w_ada, b_ada, l).reshape(b, N_MOD, d)
        w = w_in[l]
        w_k = jnp.concatenate([w[:, dh:2 * dh], w[:, 4 * dh:5 * dh]], axis=1).astype(BF16)
        w_qvt = jnp.concatenate([w[:, 0:dh], w[:, 3 * dh:4 * dh], w[:, 2 * dh:3 * dh],
                                 w[:, 5 * dh:6 * dh], w[:, 6 * dh:6 * dh + N_HEADS],
                                 jnp.zeros((d, F_PAD_ROWS - N_HEADS), w.dtype)],
                                axis=1).T.astype(BF16)
        w_g = w[:, 6 * dh + N_HEADS:].astype(BF16)
        k_sb, k_fx, qt_sb, qt_fx, vt_sb, vt_fx, lf_t, gates = _inproj_call(
            x, mod3, g_mix[l].reshape(1, d), w_k, w_qvt, w_g, b_forget[l].reshape(N_HEADS, 1),
            b_gate[l].reshape(1, 2 * d), tm, tq)

        yt_sb = _attn_call(qt_sb, k_sb, vt_sb, tq)
        aug = _cumsum_call(lf_t)
        yt_fx = _attn_call(qt_fx, k_fx, vt_fx, tq, aug)

        x = _post_call(x, yt_sb, yt_fx, gates, mod3, g_ffn[l].reshape(1, d), g_final.reshape(1, d),
                       w_branch_sb[l].astype(BF16), w_branch_fox[l].astype(BF16),
                       w_out[l].astype(BF16), w_ffn_gate[l].astype(BF16),
                       w_ffn_up[l].astype(BF16), w_ffn_down[l].astype(BF16), tm)
    return x
```

```python
import functools
import math

import jax
import jax.numpy as jnp
from jax import lax
from jax.experimental import pallas as pl
from jax.experimental.pallas import tpu as pltpu

F32 = jnp.float32
BF16 = jnp.bfloat16

HEAD_DIM = 64
N_HEADS = 8
D_HEADS = N_HEADS * HEAD_DIM
PAIR = 2 * HEAD_DIM
N_MOD = 6
RMS_EPS = 1e-6
LOG2E = math.log2(math.e)
SB_Q_SCALE = HEAD_DIM ** -0.5
FOX_Q_SCALE = LOG2E * HEAD_DIM ** -0.5
AUG = 16
DEN_ROWS = 16
F_PAD_ROWS = 16
NEG_BIG = -1e30
SB_DEAD = 160.0 * math.log(2.0)
V7X_VMEM_LIMIT = 56 * 1024 * 1024
V7X_MXU_WIDTH = 256
NT_DIMS = (((1,), (1,)), ((), ()))


def _rms_modulate(x, g, scale, shift):
    ms = jnp.mean(x * x, axis=-1, keepdims=True)
    y = x * lax.rsqrt(ms + RMS_EPS) * g
    return y * (1.0 + scale) + shift


def _const_spec(shape):
    return pl.BlockSpec(shape, lambda *_: (0,) * len(shape), pipeline_mode=pl.Buffered(1))


def _mod_kernel(c_ref, w_ref, b_ref, o_ref):
    c = c_ref[...]
    ca = c * jax.nn.sigmoid(c)
    o_ref[...] = jnp.dot(ca, w_ref[...], preferred_element_type=F32,
                         precision=lax.Precision.HIGHEST) + b_ref[...]


def _mod_call(c, w_ada, b_ada, layer):
    b, d = c.shape
    depth, _, n = w_ada.shape
    tn = 1024
    return pl.pallas_call(
        _mod_kernel,
        grid=(n // tn,),
        in_specs=[pl.BlockSpec((b, d), lambda j: (0, 0)),
                  pl.BlockSpec((None, d, tn), lambda j: (layer, 0, j)),
                  pl.BlockSpec((None, 1, tn), lambda j: (layer, 0, j))],
        out_specs=pl.BlockSpec((b, tn), lambda j: (0, j)),
        out_shape=jax.ShapeDtypeStruct((b, n), F32),
        name="mod",
    )(c, w_ada, b_ada.reshape(depth, 1, n))


def _inproj_kernel(x_ref, mod_ref, g_ref, wk_ref, wqvt_ref, wg_ref, bf_ref, bg_ref,
                   ks_ref, kf_ref, qts_ref, qtf_ref, vts_ref, vtf_ref, lft_ref, gate_ref, *, tk):
    mod = mod_ref[...]
    h = _rms_modulate(x_ref[...], g_ref[...], mod[1:2, :], mod[0:1, :]).astype(BF16)
    d = D_HEADS
    kk = jnp.dot(h, wk_ref[...], preferred_element_type=F32)
    ks_ref[...] = kk[:, :d].astype(BF16)
    kf_ref[...] = kk[:, d:].astype(BF16)
    t = lax.dot_general(wqvt_ref[...], h, NT_DIMS, preferred_element_type=F32)
    qts_ref[...] = (t[0 * d:1 * d] * SB_Q_SCALE).astype(BF16)
    qtf_ref[...] = (t[1 * d:2 * d] * FOX_Q_SCALE).astype(BF16)
    for c in range(vts_ref.shape[0]):
        vts_ref[c] = t[2 * d:3 * d, c * tk:(c + 1) * tk].astype(BF16)
        vtf_ref[c] = t[3 * d:4 * d, c * tk:(c + 1) * tk].astype(BF16)
    f = t[4 * d:4 * d + N_HEADS] + bf_ref[...]
    lft_ref[...] = -(jnp.maximum(-f, 0.0) + jnp.log(1.0 + jnp.exp(-jnp.abs(f)))) * LOG2E
    gl = jnp.dot(h, wg_ref[...], preferred_element_type=F32) + bg_ref[...]
    gate_ref[...] = jax.nn.sigmoid(gl).astype(BF16)


def _inproj_call(x, mod3, g_mix, w_k, w_qvt, w_g, b_f, b_g, tm, tk):
    b, s, d = x.shape
    row = lambda bi, i: (bi, i, 0)
    col = lambda bi, i: (bi, 0, i)
    k_spec = pl.BlockSpec((None, tm, D_HEADS), row)
    qt_spec = pl.BlockSpec((None, D_HEADS, tm), col)
    vt_spec = pl.BlockSpec((None, tm // tk, D_HEADS, tk), lambda bi, i: (bi, i, 0, 0))
    return pl.pallas_call(
        functools.partial(_inproj_kernel, tk=tk),
        grid=(b, s // tm),
        in_specs=[pl.BlockSpec((None, tm, d), row),
                  pl.BlockSpec((None, N_MOD, d), lambda bi, i: (bi, 0, 0)),
                  _const_spec((1, d)),
                  _const_spec(w_k.shape), _const_spec(w_qvt.shape),
                  _const_spec(w_g.shape), _const_spec(b_f.shape), _const_spec(b_g.shape)],
        out_specs=[k_spec, k_spec, qt_spec, qt_spec, vt_spec, vt_spec,
                   pl.BlockSpec((None, N_HEADS, tm), col),
                   pl.BlockSpec((None, tm, 2 * d), row)],
        out_shape=[jax.ShapeDtypeStruct((b, s, D_HEADS), BF16)] * 2
        + [jax.ShapeDtypeStruct((b, D_HEADS, s), BF16)] * 2
        + [jax.ShapeDtypeStruct((b, s // tk, D_HEADS, tk), BF16)] * 2
        + [jax.ShapeDtypeStruct((b, N_HEADS, s), F32),
           jax.ShapeDtypeStruct((b, s, 2 * d), BF16)],
        compiler_params=pltpu.CompilerParams(
            dimension_semantics=("parallel", "parallel"), vmem_limit_bytes=V7X_VMEM_LIMIT),
        name="inproj",
    )(x, mod3, g_mix, w_k, w_qvt, w_g, b_f, b_g)


def _split3(v):
    hi = v.astype(BF16).astype(F32)
    r = v - hi
    mid = r.astype(BF16).astype(F32)
    lo = (r - mid).astype(BF16).astype(F32)
    return hi, mid, lo


def _cumsum_kernel(lft_ref, augk_ref, augqt_ref, *, chunk):
    h, s = lft_ref.shape
    n3 = 3 * h
    r = lax.broadcasted_iota(jnp.int32, (chunk, chunk), 0)
    c = lax.broadcasted_iota(jnp.int32, (chunk, chunk), 1)
    upper = (r <= c).astype(BF16)

    qr = lax.broadcasted_iota(jnp.int32, (h * AUG, n3), 0)
    qc = lax.broadcasted_iota(jnp.int32, (h * AUG, n3), 1)
    qrow = lax.broadcasted_iota(jnp.int32, (h * AUG, 1), 0)
    place_q = jnp.zeros((h * AUG, n3), F32)
    place_k = jnp.zeros((h * AUG, n3), F32)
    ones_q = jnp.zeros((h * AUG, 1), F32)
    ones_k = jnp.zeros((h * AUG, 1), F32)
    for hh in range(h):
        for p in range(3):
            place_q = jnp.where((qr == hh * AUG + 3 + p) & (qc == p * h + hh), 1.0, place_q)
            place_k = jnp.where((qr == hh * AUG + p) & (qc == p * h + hh), -1.0, place_k)
            ones_q = jnp.where(qrow == hh * AUG + p, 1.0, ones_q)
            ones_k = jnp.where(qrow == hh * AUG + 3 + p, 1.0, ones_k)
    place_q = place_q.astype(BF16)
    place_k = place_k.astype(BF16)

    carry = jnp.zeros((h, 1), F32)
    for ci in range(s // chunk):
        sl = slice(ci * chunk, (ci + 1) * chunk)
        pieces = jnp.concatenate(_split3(lft_ref[:, sl]), axis=0).astype(BF16)
        p3 = jnp.dot(pieces, upper, preferred_element_type=F32)
        cum = (p3[0:h] + p3[h:2 * h]) + p3[2 * h:n3] + carry
        carry = cum[:, chunk - 1:chunk]
        stacked = jnp.concatenate(_split3(cum), axis=0).astype(BF16)
        augqt_ref[:, sl] = (jnp.dot(place_q, stacked, preferred_element_type=F32)
                            + ones_q).astype(BF16)
        augkt = jnp.dot(place_k, stacked, preferred_element_type=F32) + ones_k
        augk_ref[sl, :] = augkt.T.astype(BF16)


def _cumsum_call(lf_t):
    b, h, s = lf_t.shape
    return pl.pallas_call(
        functools.partial(_cumsum_kernel, chunk=min(256, s)),
        grid=(b,),
        in_specs=[pl.BlockSpec((None, h, s), lambda bi: (bi, 0, 0))],
        out_specs=[pl.BlockSpec((None, s, h * AUG), lambda bi: (bi, 0, 0)),
                   pl.BlockSpec((None, h * AUG, s), lambda bi: (bi, 0, 0))],
        out_shape=[jax.ShapeDtypeStruct((b, s, h * AUG), BF16),
                   jax.ShapeDtypeStruct((b, h * AUG, s), BF16)],
        compiler_params=pltpu.CompilerParams(dimension_semantics=("parallel",)),
        name="cumsum",
    )(lf_t)


def _softplus(z):
    return jnp.maximum(z, 0.0) + jnp.log(1.0 + jnp.exp(-jnp.abs(z)))


def _fill_query_operand(rhs_ref, qt_ref, augqt_ref, first_tile):
    @pl.when(first_tile)
    def _():
        rhs_ref[...] = jnp.zeros(rhs_ref.shape, rhs_ref.dtype)

    for hh in range(N_HEADS):
        half = hh % 2
        rhs_ref[hh, half * HEAD_DIM:(half + 1) * HEAD_DIM, :] = (
            qt_ref[hh * HEAD_DIM:(hh + 1) * HEAD_DIM, :])
        if augqt_ref is not None:
            rhs_ref[hh, PAIR + hh * AUG:PAIR + (hh + 1) * AUG, :] = (
                augqt_ref[hh * AUG:(hh + 1) * AUG, :])


def _sb_kernel(qt_ref, k_ref, vt_ref, o_ref, rhs_ref, z_ref, acc_ref, run_ref, c_ref, *, tq):
    g = N_HEADS
    i = pl.program_id(1)
    _fill_query_operand(rhs_ref, qt_ref, None, i == 0)
    rows = lax.broadcasted_iota(jnp.int32, (tq, tq), 0)
    cols = lax.broadcasted_iota(jnp.int32, (tq, tq), 1)
    suffix = (cols >= rows).astype(BF16)

    def scores(hh, j, diagonal):
        start = pl.multiple_of(j * tq, tq)
        z = jnp.dot(k_ref[pl.ds(start, tq), (hh // 2) * PAIR:(hh // 2 + 1) * PAIR], rhs_ref[hh],
                    preferred_element_type=F32)
        return jnp.where(rows < cols, z, NEG_BIG) if diagonal else z

    def block(j, j_next):
        for hh in range(g):
            sp = _softplus(z_ref[hh].astype(BF16))
            c_ref[hh] = jnp.dot(suffix, sp, preferred_element_type=F32)
        runs = []
        for hh in range(g):
            run = run_ref[hh:hh + 1, :]
            runs.append(run)
            run_ref[hh:hh + 1, :] = run + c_ref[hh, 0:1, :]
        lowest = jnp.min(run_ref[...])
        for hh in range(g):
            z_next = scores(hh, j_next, False)
            w = jnp.exp(z_ref[hh] - c_ref[hh] - runs[hh])
            acc_ref[hh] += jnp.dot(vt_ref[j, hh * HEAD_DIM:(hh + 1) * HEAD_DIM, :], w.astype(BF16),
                                   preferred_element_type=F32)
            z_ref[hh] = z_next
        return lowest

    for hh in range(g):
        z_ref[hh] = scores(hh, i, True)
    acc_ref[...] = jnp.zeros(acc_ref.shape, F32)
    run_ref[...] = jnp.zeros(run_ref.shape, F32)
    lowest = block(i, jnp.maximum(i - 1, 0))

    def cond(state):
        n, lowest = state
        return jnp.logical_and(n < i, lowest < SB_DEAD)

    def body(state):
        n, _ = state
        j = i - 1 - n
        return n + 1, block(j, jnp.maximum(j - 1, 0))

    lax.while_loop(cond, body, (jnp.int32(0), lowest))
    for hh in range(g):
        o_ref[hh * HEAD_DIM:(hh + 1) * HEAD_DIM, :] = acc_ref[hh].astype(o_ref.dtype)


def _fox_kernel(qt_ref, augqt_ref, k_ref, augk_ref, vt_ref, o_ref, rhs_ref, s_ref, acc_ref, m_ref,
                bmax_ref, *, tq):
    g = N_HEADS
    i = pl.program_id(1)
    _fill_query_operand(rhs_ref, qt_ref, augqt_ref, i == 0)
    den_rows = jnp.ones((DEN_ROWS, tq), BF16)

    def scores(hh, j, diagonal):
        start = pl.multiple_of(j * tq, tq)
        pair = slice((hh // 2) * PAIR, (hh // 2 + 1) * PAIR)
        lhs = jnp.concatenate([k_ref[pl.ds(start, tq), pair], augk_ref[pl.ds(start, tq), :]],
                              axis=1)
        s = jnp.dot(lhs, rhs_ref[hh], preferred_element_type=F32)
        if diagonal:
            rows = lax.broadcasted_iota(jnp.int32, (tq, tq), 0)
            cols = lax.broadcasted_iota(jnp.int32, (tq, tq), 1)
            s = jnp.where(rows <= cols, s, NEG_BIG)
        return s

    def keep(hh, s):
        s_ref[hh] = s
        bmax_ref[hh:hh + 1, :] = jnp.max(s, axis=0, keepdims=True)

    def block(j, j_next):
        for hh in range(g):
            s_next = scores(hh, j_next, False)
            m = m_ref[hh:hh + 1, :]
            m_new = jnp.maximum(m, bmax_ref[hh:hh + 1, :])
            alpha = jnp.exp2(m - m_new)
            p = jnp.exp2(s_ref[hh] - m_new).astype(BF16)
            m_ref[hh:hh + 1, :] = m_new
            vt = jnp.concatenate([vt_ref[j, hh * HEAD_DIM:(hh + 1) * HEAD_DIM, :], den_rows], axis=0)
            acc_ref[hh] = alpha * acc_ref[hh] + jnp.dot(vt, p, preferred_element_type=F32)
            keep(hh, s_next)

    for hh in range(g):
        keep(hh, scores(hh, i, True))
    acc_ref[...] = jnp.zeros(acc_ref.shape, F32)
    m_ref[...] = jnp.full(m_ref.shape, NEG_BIG, F32)
    block(i, 0)
    last = jnp.maximum(i - 1, 0)

    @pl.loop(0, i // 2)
    def _(n):
        block(2 * n, 2 * n + 1)
        block(2 * n + 1, jnp.minimum(2 * n + 2, last))

    @pl.when(i % 2 == 1)
    def _():
        block(last, last)

    for hh in range(g):
        o_ref[hh * HEAD_DIM:(hh + 1) * HEAD_DIM, :] = (
            acc_ref[hh, :HEAD_DIM, :] / acc_ref[hh, HEAD_DIM:HEAD_DIM + 1, :]).astype(o_ref.dtype)


def _attn_call(qt, k, vt, tq, aug=None):
    b, d, s = qt.shape
    nk = s // tq
    q_spec = pl.BlockSpec((None, d, tq), lambda bi, i: (bi, 0, i))
    k_spec = pl.BlockSpec((None, s, d), lambda bi, i: (bi, 0, 0))
    vt_spec = pl.BlockSpec((None, nk, d, tq), lambda bi, i: (bi, 0, 0, 0))
    if aug is None:
        body, name, kdim, acc_rows, n_row_states = _sb_kernel, "sb_attn", PAIR, HEAD_DIM, 1
        extra = [pltpu.VMEM((N_HEADS, tq, tq), F32)]
        operands, in_specs = (qt, k, vt), [q_spec, k_spec, vt_spec]
    else:
        augk, augqt = aug
        body, name, kdim, acc_rows = _fox_kernel, "fox_attn", 2 * PAIR, HEAD_DIM + DEN_ROWS
        n_row_states, extra = 2, []
        operands = (qt, augqt, k, augk, vt)
        in_specs = [q_spec, pl.BlockSpec((None, N_HEADS * AUG, tq), lambda bi, i: (bi, 0, i)),
                    k_spec, pl.BlockSpec((None, s, N_HEADS * AUG), lambda bi, i: (bi, 0, 0)),
                    vt_spec]
    return pl.pallas_call(
        functools.partial(body, tq=tq),
        grid=(b, s // tq),
        in_specs=in_specs,
        out_specs=q_spec,
        out_shape=jax.ShapeDtypeStruct((b, d, s), BF16),
        scratch_shapes=[pltpu.VMEM((N_HEADS, kdim, tq), BF16),
                        pltpu.VMEM((N_HEADS, tq, tq), F32),
                        pltpu.VMEM((N_HEADS, acc_rows, tq), F32)]
        + [pltpu.VMEM((N_HEADS, tq), F32)] * n_row_states + extra,
        compiler_params=pltpu.CompilerParams(
            dimension_semantics=("parallel", "arbitrary"), vmem_limit_bytes=V7X_VMEM_LIMIT),
        name=name,
    )(*operands)


def _post_kernel(x_ref, yst_ref, yft_ref, gate_ref, mod_ref, gffn_ref, gfin_ref, ws_ref, wfx_ref,
                 wo_ref, wg_ref, wu_ref, wd_ref, o_ref, *, bounds):
    d = x_ref.shape[-1]
    mod = mod_ref[...]
    a = jnp.dot(yst_ref[...].T, ws_ref[...], preferred_element_type=F32)
    b = jnp.dot(yft_ref[...].T, wfx_ref[...], preferred_element_type=F32)
    gate = gate_ref[...].astype(F32)
    merged = gate[:, :d] * a + gate[:, d:] * b
    x1 = x_ref[...] + mod[2:3, :] * jnp.dot(merged.astype(BF16), wo_ref[...],
                                            preferred_element_type=F32)
    h2 = _rms_modulate(x1, gffn_ref[...], mod[4:5, :], mod[3:4, :]).astype(BF16)
    ffn = None
    for lo, hi in zip(bounds[:-1], bounds[1:]):
        sl = slice(lo, hi)
        gt = jnp.dot(h2, wg_ref[:, sl], preferred_element_type=F32)
        up = jnp.dot(h2, wu_ref[:, sl], preferred_element_type=F32)
        act = (gt * jax.nn.sigmoid(gt) * up).astype(BF16)
        part = jnp.dot(act, wd_ref[sl, :], preferred_element_type=F32)
        ffn = part if ffn is None else ffn + part
    x2 = x1 + mod[5:6, :] * ffn
    ms = jnp.mean(x2 * x2, axis=-1, keepdims=True)
    o_ref[...] = x2 * lax.rsqrt(ms + RMS_EPS) * gfin_ref[...]


def _post_call(x, yt_sb, yt_fx, gates, mod3, g_ffn, g_final, w_bs, w_bf, w_out, w_gate, w_up,
               w_down, tm):
    b, s, d = x.shape
    row = lambda bi, i: (bi, i, 0)
    yt_spec = pl.BlockSpec((None, D_HEADS, tm), lambda bi, i: (bi, 0, i))
    dff = w_gate.shape[1]
    half = -(-dff // (2 * V7X_MXU_WIDTH)) * V7X_MXU_WIDTH
    bounds = (0, half, dff) if half < dff else (0, dff)
    weights = (w_bs, w_bf, w_out, w_gate, w_up, w_down)
    return pl.pallas_call(
        functools.partial(_post_kernel, bounds=bounds),
        grid=(b, s // tm),
        in_specs=[pl.BlockSpec((None, tm, d), row), yt_spec, yt_spec,
                  pl.BlockSpec((None, tm, 2 * d), row),
                  pl.BlockSpec((None, N_MOD, d), lambda bi, i: (bi, 0, 0)),
                  _const_spec((1, d)), _const_spec((1, d))]
        + [_const_spec(w.shape) for w in weights],
        out_specs=pl.BlockSpec((None, tm, d), row),
        out_shape=jax.ShapeDtypeStruct((b, s, d), F32),
        compiler_params=pltpu.CompilerParams(
            dimension_semantics=("parallel", "parallel"), vmem_limit_bytes=V7X_VMEM_LIMIT),
        name="post",
    )(x, yt_sb, yt_fx, gates, mod3, g_ffn, g_final, *weights)


def kernel(x, c, w_ada, b_ada, g_mix, w_in, b_forget, b_gate, w_branch_sb, w_branch_fox,
           w_out, g_ffn, w_ffn_gate, w_ffn_up, w_ffn_down, g_final):
    b, s, d = x.shape
    depth = w_ada.shape[0]
    assert depth == 1, "the final RMSNorm is fused into the last (only) layer's closing call"
    tm = min(512, s)
    tq = min(256, s)
    dh = D_HEADS
    for l in range(depth):
        mod3 = _mod_call(c, w_ada, b_ada, l).reshape(b, N_MOD, d)
        w = w_in[l]
        w_k = jnp.concatenate([w[:, dh:2 * dh], w[:, 4 * dh:5 * dh]], axis=1).astype(BF16)
        w_qvt = jnp.concatenate([w[:, 0:dh], w[:, 3 * dh:4 * dh], w[:, 2 * dh:3 * dh],
                                 w[:, 5 * dh:6 * dh], w[:, 6 * dh:6 * dh + N_HEADS],
                                 jnp.zeros((d, F_PAD_ROWS - N_HEADS), w.dtype)],
                                axis=1).T.astype(BF16)
        w_g = w[:, 6 * dh + N_HEADS:].astype(BF16)
        k_sb, k_fx, qt_sb, qt_fx, vt_sb, vt_fx, lf_t, gates = _inproj_call(
            x, mod3, g_mix[l].reshape(1, d), w_k, w_qvt, w_g, b_forget[l].reshape(N_HEADS, 1),
            b_gate[l].reshape(1, 2 * d), tm, tq)

        yt_sb = _attn_call(qt_sb, k_sb, vt_sb, tq)
        aug = _cumsum_call(lf_t)
        yt_fx = _attn_call(qt_fx, k_fx, vt_fx, tq, aug)

        x = _post_call(x, yt_sb, yt_fx, gates, mod3, g_ffn[l].reshape(1, d), g_final.reshape(1, d),
                       w_branch_sb[l].astype(BF16), w_branch_fox[l].astype(BF16),
                       w_out[l].astype(BF16), w_ffn_gate[l].astype(BF16),
                       w_ffn_up[l].astype(BF16), w_ffn_down[l].astype(BF16), tm)
    return x
```

```python
import functools
import math

import jax
import jax.numpy as jnp
from jax import lax
from jax.experimental import pallas as pl
from jax.experimental.pallas import tpu as pltpu

F32 = jnp.float32
BF16 = jnp.bfloat16

HEAD_DIM = 64
N_HEADS = 8
D_HEADS = N_HEADS * HEAD_DIM
PAIR = 2 * HEAD_DIM
N_MOD = 6
RMS_EPS = 1e-6
LOG2E = math.log2(math.e)
SB_Q_SCALE = HEAD_DIM ** -0.5
FOX_Q_SCALE = LOG2E * HEAD_DIM ** -0.5
AUG = 16
DEN_ROWS = 16
WEIGHT_LAG = 3
F_PAD_ROWS = 16
NEG_BIG = -1e30
SB_DEAD = 160.0 * math.log(2.0)
V7X_VMEM_LIMIT = 56 * 1024 * 1024
V7X_MXU_WIDTH = 256
NT_DIMS = (((1,), (1,)), ((), ()))


def _rms_modulate(x, g, scale, shift):
    ms = jnp.mean(x * x, axis=-1, keepdims=True)
    y = x * lax.rsqrt(ms + RMS_EPS) * g
    return y * (1.0 + scale) + shift


def _const_spec(shape):
    return pl.BlockSpec(shape, lambda *_: (0,) * len(shape), pipeline_mode=pl.Buffered(1))


def _mod_kernel(c_ref, w_ref, b_ref, o_ref):
    c = c_ref[...]
    ca = c * jax.nn.sigmoid(c)
    o_ref[...] = jnp.dot(ca, w_ref[...], preferred_element_type=F32,
                         precision=lax.Precision.HIGHEST) + b_ref[...]


def _mod_call(c, w_ada, b_ada, layer):
    b, d = c.shape
    depth, _, n = w_ada.shape
    tn = 1024
    return pl.pallas_call(
        _mod_kernel,
        grid=(n // tn,),
        in_specs=[pl.BlockSpec((b, d), lambda j: (0, 0)),
                  pl.BlockSpec((None, d, tn), lambda j: (layer, 0, j)),
                  pl.BlockSpec((None, 1, tn), lambda j: (layer, 0, j))],
        out_specs=pl.BlockSpec((b, tn), lambda j: (0, j)),
        out_shape=jax.ShapeDtypeStruct((b, n), F32),
        name="mod",
    )(c, w_ada, b_ada.reshape(depth, 1, n))


def _inproj_kernel(x_ref, mod_ref, g_ref, wk_ref, wqvt_ref, wg_ref, bf_ref, bg_ref,
                   ks_ref, kf_ref, qts_ref, qtf_ref, vts_ref, vtf_ref, lft_ref, gate_ref, *, tk):
    mod = mod_ref[...]
    h = _rms_modulate(x_ref[...], g_ref[...], mod[1:2, :], mod[0:1, :]).astype(BF16)
    d = D_HEADS
    kk = jnp.dot(h, wk_ref[...], preferred_element_type=F32)
    ks_ref[...] = kk[:, :d].astype(BF16)
    kf_ref[...] = kk[:, d:].astype(BF16)
    t = lax.dot_general(wqvt_ref[...], h, NT_DIMS, preferred_element_type=F32)
    qts_ref[...] = (t[0 * d:1 * d] * SB_Q_SCALE).astype(BF16)
    qtf_ref[...] = (t[1 * d:2 * d] * FOX_Q_SCALE).astype(BF16)
    for c in range(vts_ref.shape[0]):
        vts_ref[c] = t[2 * d:3 * d, c * tk:(c + 1) * tk].astype(BF16)
        vtf_ref[c] = t[3 * d:4 * d, c * tk:(c + 1) * tk].astype(BF16)
    f = t[4 * d:4 * d + N_HEADS] + bf_ref[...]
    lft_ref[...] = -(jnp.maximum(-f, 0.0) + jnp.log(1.0 + jnp.exp(-jnp.abs(f)))) * LOG2E
    gl = jnp.dot(h, wg_ref[...], preferred_element_type=F32) + bg_ref[...]
    gate_ref[...] = jax.nn.sigmoid(gl).astype(BF16)


def _inproj_call(x, mod3, g_mix, w_k, w_qvt, w_g, b_f, b_g, tm, tk):
    b, s, d = x.shape
    row = lambda bi, i: (bi, i, 0)
    col = lambda bi, i: (bi, 0, i)
    k_spec = pl.BlockSpec((None, tm, D_HEADS), row)
    qt_spec = pl.BlockSpec((None, D_HEADS, tm), col)
    vt_spec = pl.BlockSpec((None, tm // tk, D_HEADS, tk), lambda bi, i: (bi, i, 0, 0))
    return pl.pallas_call(
        functools.partial(_inproj_kernel, tk=tk),
        grid=(b, s // tm),
        in_specs=[pl.BlockSpec((None, tm, d), row),
                  pl.BlockSpec((None, N_MOD, d), lambda bi, i: (bi, 0, 0)),
                  _const_spec((1, d)),
                  _const_spec(w_k.shape), _const_spec(w_qvt.shape),
                  _const_spec(w_g.shape), _const_spec(b_f.shape), _const_spec(b_g.shape)],
        out_specs=[k_spec, k_spec, qt_spec, qt_spec, vt_spec, vt_spec,
                   pl.BlockSpec((None, N_HEADS, tm), col),
                   pl.BlockSpec((None, tm, 2 * d), row)],
        out_shape=[jax.ShapeDtypeStruct((b, s, D_HEADS), BF16)] * 2
        + [jax.ShapeDtypeStruct((b, D_HEADS, s), BF16)] * 2
        + [jax.ShapeDtypeStruct((b, s // tk, D_HEADS, tk), BF16)] * 2
        + [jax.ShapeDtypeStruct((b, N_HEADS, s), F32),
           jax.ShapeDtypeStruct((b, s, 2 * d), BF16)],
        compiler_params=pltpu.CompilerParams(
            dimension_semantics=("parallel", "parallel"), vmem_limit_bytes=V7X_VMEM_LIMIT),
        name="inproj",
    )(x, mod3, g_mix, w_k, w_qvt, w_g, b_f, b_g)


def _split3(v):
    hi = v.astype(BF16).astype(F32)
    r = v - hi
    mid = r.astype(BF16).astype(F32)
    lo = (r - mid).astype(BF16).astype(F32)
    return hi, mid, lo


def _cumsum_kernel(lft_ref, augk_ref, augqt_ref, *, chunk):
    h, s = lft_ref.shape
    n3 = 3 * h
    r = lax.broadcasted_iota(jnp.int32, (chunk, chunk), 0)
    c = lax.broadcasted_iota(jnp.int32, (chunk, chunk), 1)
    upper = (r <= c).astype(BF16)

    qr = lax.broadcasted_iota(jnp.int32, (h * AUG, n3), 0)
    qc = lax.broadcasted_iota(jnp.int32, (h * AUG, n3), 1)
    qrow = lax.broadcasted_iota(jnp.int32, (h * AUG, 1), 0)
    place_q = jnp.zeros((h * AUG, n3), F32)
    place_k = jnp.zeros((h * AUG, n3), F32)
    ones_q = jnp.zeros((h * AUG, 1), F32)
    ones_k = jnp.zeros((h * AUG, 1), F32)
    for hh in range(h):
        for p in range(3):
            place_q = jnp.where((qr == hh * AUG + 3 + p) & (qc == p * h + hh), 1.0, place_q)
            place_k = jnp.where((qr == hh * AUG + p) & (qc == p * h + hh), -1.0, place_k)
            ones_q = jnp.where(qrow == hh * AUG + p, 1.0, ones_q)
            ones_k = jnp.where(qrow == hh * AUG + 3 + p, 1.0, ones_k)
    place_q = place_q.astype(BF16)
    place_k = place_k.astype(BF16)

    carry = jnp.zeros((h, 1), F32)
    for ci in range(s // chunk):
        sl = slice(ci * chunk, (ci + 1) * chunk)
        pieces = jnp.concatenate(_split3(lft_ref[:, sl]), axis=0).astype(BF16)
        p3 = jnp.dot(pieces, upper, preferred_element_type=F32)
        cum = (p3[0:h] + p3[h:2 * h]) + p3[2 * h:n3] + carry
        carry = cum[:, chunk - 1:chunk]
        stacked = jnp.concatenate(_split3(cum), axis=0).astype(BF16)
        augqt_ref[:, sl] = (jnp.dot(place_q, stacked, preferred_element_type=F32)
                            + ones_q).astype(BF16)
        augkt = jnp.dot(place_k, stacked, preferred_element_type=F32) + ones_k
        augk_ref[sl, :] = augkt.T.astype(BF16)


def _cumsum_call(lf_t):
    b, h, s = lf_t.shape
    return pl.pallas_call(
        functools.partial(_cumsum_kernel, chunk=min(256, s)),
        grid=(b,),
        in_specs=[pl.BlockSpec((None, h, s), lambda bi: (bi, 0, 0))],
        out_specs=[pl.BlockSpec((None, s, h * AUG), lambda bi: (bi, 0, 0)),
                   pl.BlockSpec((None, h * AUG, s), lambda bi: (bi, 0, 0))],
        out_shape=[jax.ShapeDtypeStruct((b, s, h * AUG), BF16),
                   jax.ShapeDtypeStruct((b, h * AUG, s), BF16)],
        compiler_params=pltpu.CompilerParams(dimension_semantics=("parallel",)),
        name="cumsum",
    )(lf_t)


def _softplus(z):
    return jnp.maximum(z, 0.0) + jnp.log(1.0 + jnp.exp(-jnp.abs(z)))


def _fill_query_operand(rhs_ref, qt_ref, augqt_ref, first_tile):
    @pl.when(first_tile)
    def _():
        rhs_ref[...] = jnp.zeros(rhs_ref.shape, rhs_ref.dtype)

    for hh in range(N_HEADS):
        half = hh % 2
        rhs_ref[hh, half * HEAD_DIM:(half + 1) * HEAD_DIM, :] = (
            qt_ref[hh * HEAD_DIM:(hh + 1) * HEAD_DIM, :])
        if augqt_ref is not None:
            rhs_ref[hh, PAIR + hh * AUG:PAIR + (hh + 1) * AUG, :] = (
                augqt_ref[hh * AUG:(hh + 1) * AUG, :])


def _sb_kernel(qt_ref, k_ref, vt_ref, o_ref, rhs_ref, z_ref, acc_ref, run_ref, c_ref, *, tq):
    g = N_HEADS
    i = pl.program_id(1)
    _fill_query_operand(rhs_ref, qt_ref, None, i == 0)
    rows = lax.broadcasted_iota(jnp.int32, (tq, tq), 0)
    cols = lax.broadcasted_iota(jnp.int32, (tq, tq), 1)
    suffix = (cols >= rows).astype(BF16)

    def scores(hh, j, diagonal):
        start = pl.multiple_of(j * tq, tq)
        z = jnp.dot(k_ref[pl.ds(start, tq), (hh // 2) * PAIR:(hh // 2 + 1) * PAIR], rhs_ref[hh],
                    preferred_element_type=F32)
        return jnp.where(rows < cols, z, NEG_BIG) if diagonal else z

    def block(j, j_next):
        runs = [run_ref[hh:hh + 1, :] for hh in range(g)]
        for t in range(g + WEIGHT_LAG):
            if t < g:
                sp = _softplus(z_ref[t].astype(BF16))
                c_ref[t] = jnp.dot(suffix, sp, preferred_element_type=F32)
            if t >= WEIGHT_LAG:
                hh = t - WEIGHT_LAG
                z_next = scores(hh, j_next, False)
                w = jnp.exp(z_ref[hh] - c_ref[hh] - runs[hh])
                acc_ref[hh] += jnp.dot(vt_ref[j, hh * HEAD_DIM:(hh + 1) * HEAD_DIM, :],
                                       w.astype(BF16), preferred_element_type=F32)
                run_ref[hh:hh + 1, :] = runs[hh] + c_ref[hh, 0:1, :]
                z_ref[hh] = z_next
        return jnp.min(run_ref[...])

    for hh in range(g):
        z_ref[hh] = scores(hh, i, True)
    acc_ref[...] = jnp.zeros(acc_ref.shape, F32)
    run_ref[...] = jnp.zeros(run_ref.shape, F32)
    lowest = block(i, jnp.maximum(i - 1, 0))

    def cond(state):
        n, lowest = state
        return jnp.logical_and(n < i, lowest < SB_DEAD)

    def body(state):
        n, _ = state
        j = i - 1 - n
        return n + 1, block(j, jnp.maximum(j - 1, 0))

    lax.while_loop(cond, body, (jnp.int32(0), lowest))
    for hh in range(g):
        o_ref[hh * HEAD_DIM:(hh + 1) * HEAD_DIM, :] = acc_ref[hh].astype(o_ref.dtype)


def _fox_kernel(qt_ref, augqt_ref, k_ref, augk_ref, vt_ref, o_ref, rhs_ref, s_ref, acc_ref, m_ref,
                bmax_ref, *, tq):
    g = N_HEADS
    i = pl.program_id(1)
    _fill_query_operand(rhs_ref, qt_ref, augqt_ref, i == 0)
    den_rows = jnp.ones((DEN_ROWS, tq), BF16)

    def scores(hh, j, diagonal):
        start = pl.multiple_of(j * tq, tq)
        pair = slice((hh // 2) * PAIR, (hh // 2 + 1) * PAIR)
        lhs = jnp.concatenate([k_ref[pl.ds(start, tq), pair], augk_ref[pl.ds(start, tq), :]],
                              axis=1)
        s = jnp.dot(lhs, rhs_ref[hh], preferred_element_type=F32)
        if diagonal:
            rows = lax.broadcasted_iota(jnp.int32, (tq, tq), 0)
            cols = lax.broadcasted_iota(jnp.int32, (tq, tq), 1)
            s = jnp.where(rows <= cols, s, NEG_BIG)
        return s

    def keep(hh, s):
        s_ref[hh] = s
        bmax_ref[hh:hh + 1, :] = jnp.max(s, axis=0, keepdims=True)

    def block(j, j_next):
        for hh in range(g):
            s_next = scores(hh, j_next, False)
            m = m_ref[hh:hh + 1, :]
            m_new = jnp.maximum(m, bmax_ref[hh:hh + 1, :])
            alpha = jnp.exp2(m - m_new)
            p = jnp.exp2(s_ref[hh] - m_new).astype(BF16)
            m_ref[hh:hh + 1, :] = m_new
            vt = jnp.concatenate([vt_ref[j, hh * HEAD_DIM:(hh + 1) * HEAD_DIM, :], den_rows], axis=0)
            acc_ref[hh] = alpha * acc_ref[hh] + jnp.dot(vt, p, preferred_element_type=F32)
            keep(hh, s_next)

    for hh in range(g):
        keep(hh, scores(hh, i, True))
    acc_ref[...] = jnp.zeros(acc_ref.shape, F32)
    m_ref[...] = jnp.full(m_ref.shape, NEG_BIG, F32)
    block(i, 0)
    last = jnp.maximum(i - 1, 0)

    @pl.loop(0, i // 2)
    def _(n):
        block(2 * n, 2 * n + 1)
        block(2 * n + 1, jnp.minimum(2 * n + 2, last))

    @pl.when(i % 2 == 1)
    def _():
        block(last, last)

    for hh in range(g):
        o_ref[hh * HEAD_DIM:(hh + 1) * HEAD_DIM, :] = (
            acc_ref[hh, :HEAD_DIM, :] / acc_ref[hh, HEAD_DIM:HEAD_DIM + 1, :]).astype(o_ref.dtype)


def _attn_call(qt, k, vt, tq, aug=None):
    b, d, s = qt.shape
    nk = s // tq
    q_spec = pl.BlockSpec((None, d, tq), lambda bi, i: (bi, 0, i))
    k_spec = pl.BlockSpec((None, s, d), lambda bi, i: (bi, 0, 0))
    vt_spec = pl.BlockSpec((None, nk, d, tq), lambda bi, i: (bi, 0, 0, 0))
    if aug is None:
        body, name, kdim, acc_rows, n_row_states = _sb_kernel, "sb_attn", PAIR, HEAD_DIM, 1
        extra = [pltpu.VMEM((N_HEADS, tq, tq), F32)]
        operands, in_specs = (qt, k, vt), [q_spec, k_spec, vt_spec]
    else:
        augk, augqt = aug
        body, name, kdim, acc_rows = _fox_kernel, "fox_attn", 2 * PAIR, HEAD_DIM + DEN_ROWS
        n_row_states, extra = 2, []
        operands = (qt, augqt, k, augk, vt)
        in_specs = [q_spec, pl.BlockSpec((None, N_HEADS * AUG, tq), lambda bi, i: (bi, 0, i)),
                    k_spec, pl.BlockSpec((None, s, N_HEADS * AUG), lambda bi, i: (bi, 0, 0)),
                    vt_spec]
    return pl.pallas_call(
        functools.partial(body, tq=tq),
        grid=(b, s // tq),
        in_specs=in_specs,
        out_specs=q_spec,
        out_shape=jax.ShapeDtypeStruct((b, d, s), BF16),
        scratch_shapes=[pltpu.VMEM((N_HEADS, kdim, tq), BF16),
                        pltpu.VMEM((N_HEADS, tq, tq), F32),
                        pltpu.VMEM((N_HEADS, acc_rows, tq), F32)]
        + [pltpu.VMEM((N_HEADS, tq), F32)] * n_row_states + extra,
        compiler_params=pltpu.CompilerParams(
            dimension_semantics=("parallel", "arbitrary"), vmem_limit_bytes=V7X_VMEM_LIMIT),
        name=name,
    )(*operands)


def _post_kernel(x_ref, yst_ref, yft_ref, gate_ref, mod_ref, gffn_ref, gfin_ref, ws_ref, wfx_ref,
                 wo_ref, wg_ref, wu_ref, wd_ref, o_ref, *, bounds):
    d = x_ref.shape[-1]
    mod = mod_ref[...]
    a = jnp.dot(yst_ref[...].T, ws_ref[...], preferred_element_type=F32)
    b = jnp.dot(yft_ref[...].T, wfx_ref[...], preferred_element_type=F32)
    gate = gate_ref[...].astype(F32)
    merged = gate[:, :d] * a + gate[:, d:] * b
    x1 = x_ref[...] + mod[2:3, :] * jnp.dot(merged.astype(BF16), wo_ref[...],
                                            preferred_element_type=F32)
    h2 = _rms_modulate(x1, gffn_ref[...], mod[4:5, :], mod[3:4, :]).astype(BF16)
    ffn = None
    for lo, hi in zip(bounds[:-1], bounds[1:]):
        sl = slice(lo, hi)
        gt = jnp.dot(h2, wg_ref[:, sl], preferred_element_type=F32)
        up = jnp.dot(h2, wu_ref[:, sl], preferred_element_type=F32)
        act = (gt * jax.nn.sigmoid(gt) * up).astype(BF16)
        part = jnp.dot(act, wd_ref[sl, :], preferred_element_type=F32)
        ffn = part if ffn is None else ffn + part
    x2 = x1 + mod[5:6, :] * ffn
    ms = jnp.mean(x2 * x2, axis=-1, keepdims=True)
    o_ref[...] = x2 * lax.rsqrt(ms + RMS_EPS) * gfin_ref[...]


def _post_call(x, yt_sb, yt_fx, gates, mod3, g_ffn, g_final, w_bs, w_bf, w_out, w_gate, w_up,
               w_down, tm):
    b, s, d = x.shape
    row = lambda bi, i: (bi, i, 0)
    yt_spec = pl.BlockSpec((None, D_HEADS, tm), lambda bi, i: (bi, 0, i))
    dff = w_gate.shape[1]
    half = -(-dff // (2 * V7X_MXU_WIDTH)) * V7X_MXU_WIDTH
    bounds = (0, half, dff) if half < dff else (0, dff)
    weights = (w_bs, w_bf, w_out, w_gate, w_up, w_down)
    return pl.pallas_call(
        functools.partial(_post_kernel, bounds=bounds),
        grid=(b, s // tm),
        in_specs=[pl.BlockSpec((None, tm, d), row), yt_spec, yt_spec,
                  pl.BlockSpec((None, tm, 2 * d), row),
                  pl.BlockSpec((None, N_MOD, d), lambda bi, i: (bi, 0, 0)),
                  _const_spec((1, d)), _const_spec((1, d))]
        + [_const_spec(w.shape) for w in weights],
        out_specs=pl.BlockSpec((None, tm, d), row),
        out_shape=jax.ShapeDtypeStruct((b, s, d), F32),
        compiler_params=pltpu.CompilerParams(
            dimension_semantics=("parallel", "parallel"), vmem_limit_bytes=V7X_VMEM_LIMIT),
        name="post",
    )(x, yt_sb, yt_fx, gates, mod3, g_ffn, g_final, *weights)


def kernel(x, c, w_ada, b_ada, g_mix, w_in, b_forget, b_gate, w_branch_sb, w_branch_fox,
           w_out, g_ffn, w_ffn_gate, w_ffn_up, w_ffn_down, g_final):
    b, s, d = x.shape
    depth = w_ada.shape[0]
    assert depth == 1, "the final RMSNorm is fused into the last (only) layer's closing call"
    tm = min(512, s)
    tq = min(256, s)
    dh = D_HEADS
    for l in range(depth):
        mod3 = _mod_call(c, w_ada, b_ada, l).reshape(b, N_MOD, d)
        w = w_in[l]
        w_k = jnp.concatenate([w[:, dh:2 * dh], w[:, 4 * dh:5 * dh]], axis=1).astype(BF16)
        w_qvt = jnp.concatenate([w[:, 0:dh], w[:, 3 * dh:4 * dh], w[:, 2 * dh:3 * dh],
                                 w[:, 5 * dh:6 * dh], w[:, 6 * dh:6 * dh + N_HEADS],
                                 jnp.zeros((d, F_PAD_ROWS - N_HEADS), w.dtype)],
                                axis=1).T.astype(BF16)
        w_g = w[:, 6 * dh + N_HEADS:].astype(BF16)
        k_sb, k_fx, qt_sb, qt_fx, vt_sb, vt_fx, lf_t, gates = _inproj_call(
            x, mod3, g_mix[l].reshape(1, d), w_k, w_qvt, w_g, b_forget[l].reshape(N_HEADS, 1),
            b_gate[l].reshape(1, 2 * d), tm, tq)

        yt_sb = _attn_call(qt_sb, k_sb, vt_sb, tq)
        aug = _cumsum_call(lf_t)
        yt_fx = _attn_call(qt_fx, k_fx, vt_fx, tq, aug)

        x = _post_call(x, yt_sb, yt_fx, gates, mod3, g_ffn[l].reshape(1, d), g_final.reshape(1, d),
                       w_branch_sb[l].astype(BF16), w_branch_fox[l].astype(BF16),
                       w_out[l].astype(BF16), w_ffn_gate[l].astype(BF16),
                       w_ffn_up[l].astype(BF16), w_ffn_down[l].astype(BF16), tm)
    return x
```

```python
import functools
import math

import jax
import jax.numpy as jnp
from jax import lax
from jax.experimental import pallas as pl
from jax.experimental.pallas import tpu as pltpu

F32 = jnp.float32
BF16 = jnp.bfloat16

HEAD_DIM = 64
N_HEADS = 8
D_HEADS = N_HEADS * HEAD_DIM
PAIR = 2 * HEAD_DIM
N_MOD = 6
RMS_EPS = 1e-6
LOG2E = math.log2(math.e)
SB_Q_SCALE = HEAD_DIM ** -0.5
FOX_Q_SCALE = LOG2E * HEAD_DIM ** -0.5
AUG = 16
DEN_ROWS = 16
WEIGHT_LAG = 3
F_PAD_ROWS = 16
NEG_BIG = -1e30
SB_DEAD = 160.0 * math.log(2.0)
V7X_VMEM_LIMIT = 56 * 1024 * 1024
V7X_MXU_WIDTH = 256
NT_DIMS = (((1,), (1,)), ((), ()))


def _rms_modulate(x, g, scale, shift):
    ms = jnp.mean(x * x, axis=-1, keepdims=True)
    y = x * lax.rsqrt(ms + RMS_EPS) * g
    return y * (1.0 + scale) + shift


def _const_spec(shape):
    return pl.BlockSpec(shape, lambda *_: (0,) * len(shape), pipeline_mode=pl.Buffered(1))


def _mod_kernel(c_ref, w_ref, b_ref, o_ref):
    c = c_ref[...]
    ca = c * jax.nn.sigmoid(c)
    o_ref[...] = jnp.dot(ca, w_ref[...], preferred_element_type=F32,
                         precision=lax.Precision.HIGHEST) + b_ref[...]


def _mod_call(c, w_ada, b_ada, layer):
    b, d = c.shape
    depth, _, n = w_ada.shape
    tn = 1024
    return pl.pallas_call(
        _mod_kernel,
        grid=(n // tn,),
        in_specs=[pl.BlockSpec((b, d), lambda j: (0, 0)),
                  pl.BlockSpec((None, d, tn), lambda j: (layer, 0, j)),
                  pl.BlockSpec((None, 1, tn), lambda j: (layer, 0, j))],
        out_specs=pl.BlockSpec((b, tn), lambda j: (0, j)),
        out_shape=jax.ShapeDtypeStruct((b, n), F32),
        name="mod",
    )(c, w_ada, b_ada.reshape(depth, 1, n))


def _inproj_kernel(x_ref, mod_ref, g_ref, wk_ref, wqvt_ref, wg_ref, bf_ref, bg_ref,
                   ks_ref, kf_ref, qts_ref, qtf_ref, vts_ref, vtf_ref, lft_ref, gate_ref, *, tk):
    mod = mod_ref[...]
    h = _rms_modulate(x_ref[...], g_ref[...], mod[1:2, :], mod[0:1, :]).astype(BF16)
    d = D_HEADS
    kk = jnp.dot(h, wk_ref[...], preferred_element_type=F32)
    ks_ref[...] = kk[:, :d].astype(BF16)
    kf_ref[...] = kk[:, d:].astype(BF16)
    t = lax.dot_general(wqvt_ref[...], h, NT_DIMS, preferred_element_type=F32)
    qts_ref[...] = (t[0 * d:1 * d] * SB_Q_SCALE).astype(BF16)
    qtf_ref[...] = (t[1 * d:2 * d] * FOX_Q_SCALE).astype(BF16)
    for c in range(vts_ref.shape[0]):
        vts_ref[c] = t[2 * d:3 * d, c * tk:(c + 1) * tk].astype(BF16)
        vtf_ref[c] = t[3 * d:4 * d, c * tk:(c + 1) * tk].astype(BF16)
    f = t[4 * d:4 * d + N_HEADS] + bf_ref[...]
    lft_ref[...] = -(jnp.maximum(-f, 0.0) + jnp.log(1.0 + jnp.exp(-jnp.abs(f)))) * LOG2E
    gl = jnp.dot(h, wg_ref[...], preferred_element_type=F32) + bg_ref[...]
    gate_ref[...] = jax.nn.sigmoid(gl).astype(BF16)


def _inproj_call(x, mod3, g_mix, w_k, w_qvt, w_g, b_f, b_g, tm, tk):
    b, s, d = x.shape
    row = lambda bi, i: (bi, i, 0)
    col = lambda bi, i: (bi, 0, i)
    k_spec = pl.BlockSpec((None, tm, D_HEADS), row)
    qt_spec = pl.BlockSpec((None, D_HEADS, tm), col)
    vt_spec = pl.BlockSpec((None, tm // tk, D_HEADS, tk), lambda bi, i: (bi, i, 0, 0))
    return pl.pallas_call(
        functools.partial(_inproj_kernel, tk=tk),
        grid=(b, s // tm),
        in_specs=[pl.BlockSpec((None, tm, d), row),
                  pl.BlockSpec((None, N_MOD, d), lambda bi, i: (bi, 0, 0)),
                  _const_spec((1, d)),
                  _const_spec(w_k.shape), _const_spec(w_qvt.shape),
                  _const_spec(w_g.shape), _const_spec(b_f.shape), _const_spec(b_g.shape)],
        out_specs=[k_spec, k_spec, qt_spec, qt_spec, vt_spec, vt_spec,
                   pl.BlockSpec((None, N_HEADS, tm), col),
                   pl.BlockSpec((None, tm, 2 * d), row)],
        out_shape=[jax.ShapeDtypeStruct((b, s, D_HEADS), BF16)] * 2
        + [jax.ShapeDtypeStruct((b, D_HEADS, s), BF16)] * 2
        + [jax.ShapeDtypeStruct((b, s // tk, D_HEADS, tk), BF16)] * 2
        + [jax.ShapeDtypeStruct((b, N_HEADS, s), F32),
           jax.ShapeDtypeStruct((b, s, 2 * d), BF16)],
        compiler_params=pltpu.CompilerParams(
            dimension_semantics=("parallel", "parallel"), vmem_limit_bytes=V7X_VMEM_LIMIT),
        name="inproj",
    )(x, mod3, g_mix, w_k, w_qvt, w_g, b_f, b_g)


def _split3(v):
    hi = v.astype(BF16).astype(F32)
    r = v - hi
    mid = r.astype(BF16).astype(F32)
    lo = (r - mid).astype(BF16).astype(F32)
    return hi, mid, lo


def _cumsum_kernel(lft_ref, augk_ref, augqt_ref, *, chunk):
    h, s = lft_ref.shape
    n3 = 3 * h
    r = lax.broadcasted_iota(jnp.int32, (chunk, chunk), 0)
    c = lax.broadcasted_iota(jnp.int32, (chunk, chunk), 1)
    upper = (r <= c).astype(BF16)

    qr = lax.broadcasted_iota(jnp.int32, (h * AUG, n3), 0)
    qc = lax.broadcasted_iota(jnp.int32, (h * AUG, n3), 1)
    qrow = lax.broadcasted_iota(jnp.int32, (h * AUG, 1), 0)
    place_q = jnp.zeros((h * AUG, n3), F32)
    place_k = jnp.zeros((h * AUG, n3), F32)
    ones_q = jnp.zeros((h * AUG, 1), F32)
    ones_k = jnp.zeros((h * AUG, 1), F32)
    for hh in range(h):
        for p in range(3):
            place_q = jnp.where((qr == hh * AUG + 3 + p) & (qc == p * h + hh), 1.0, place_q)
            place_k = jnp.where((qr == hh * AUG + p) & (qc == p * h + hh), -1.0, place_k)
            ones_q = jnp.where(qrow == hh * AUG + p, 1.0, ones_q)
            ones_k = jnp.where(qrow == hh * AUG + 3 + p, 1.0, ones_k)
    place_q = place_q.astype(BF16)
    place_k = place_k.astype(BF16)

    carry = jnp.zeros((h, 1), F32)
    for ci in range(s // chunk):
        sl = slice(ci * chunk, (ci + 1) * chunk)
        pieces = jnp.concatenate(_split3(lft_ref[:, sl]), axis=0).astype(BF16)
        p3 = jnp.dot(pieces, upper, preferred_element_type=F32)
        cum = (p3[0:h] + p3[h:2 * h]) + p3[2 * h:n3] + carry
        carry = cum[:, chunk - 1:chunk]
        stacked = jnp.concatenate(_split3(cum), axis=0).astype(BF16)
        augqt_ref[:, sl] = (jnp.dot(place_q, stacked, preferred_element_type=F32)
                            + ones_q).astype(BF16)
        augkt = jnp.dot(place_k, stacked, preferred_element_type=F32) + ones_k
        augk_ref[sl, :] = augkt.T.astype(BF16)


def _cumsum_call(lf_t):
    b, h, s = lf_t.shape
    return pl.pallas_call(
        functools.partial(_cumsum_kernel, chunk=min(256, s)),
        grid=(b,),
        in_specs=[pl.BlockSpec((None, h, s), lambda bi: (bi, 0, 0))],
        out_specs=[pl.BlockSpec((None, s, h * AUG), lambda bi: (bi, 0, 0)),
                   pl.BlockSpec((None, h * AUG, s), lambda bi: (bi, 0, 0))],
        out_shape=[jax.ShapeDtypeStruct((b, s, h * AUG), BF16),
                   jax.ShapeDtypeStruct((b, h * AUG, s), BF16)],
        compiler_params=pltpu.CompilerParams(dimension_semantics=("parallel",)),
        name="cumsum",
    )(lf_t)


def _softplus(z):
    return jnp.maximum(z, 0.0) + jnp.log(1.0 + jnp.exp(-jnp.abs(z)))


def _fill_query_operand(rhs_ref, qt_ref, augqt_ref, first_tile):
    @pl.when(first_tile)
    def _():
        rhs_ref[...] = jnp.zeros(rhs_ref.shape, rhs_ref.dtype)

    for hh in range(N_HEADS):
        half = hh % 2
        rhs_ref[hh, half * HEAD_DIM:(half + 1) * HEAD_DIM, :] = (
            qt_ref[hh * HEAD_DIM:(hh + 1) * HEAD_DIM, :])
        if augqt_ref is not None:
            rhs_ref[hh, PAIR + hh * AUG:PAIR + (hh + 1) * AUG, :] = (
                augqt_ref[hh * AUG:(hh + 1) * AUG, :])


def _sb_kernel(qt_ref, k_ref, vt_ref, o_ref, rhs_ref, z_ref, acc_ref, run_ref, c_ref, *, tq):
    g = N_HEADS
    i = pl.program_id(1)
    _fill_query_operand(rhs_ref, qt_ref, None, i == 0)
    rows = lax.broadcasted_iota(jnp.int32, (tq, tq), 0)
    cols = lax.broadcasted_iota(jnp.int32, (tq, tq), 1)
    suffix = (cols >= rows).astype(BF16)

    def scores(hh, j, diagonal):
        start = pl.multiple_of(j * tq, tq)
        z = jnp.dot(k_ref[pl.ds(start, tq), (hh // 2) * PAIR:(hh // 2 + 1) * PAIR], rhs_ref[hh],
                    preferred_element_type=F32)
        return jnp.where(rows < cols, z, NEG_BIG) if diagonal else z

    def block(j, j_next):
        runs = [run_ref[hh:hh + 1, :] for hh in range(g)]
        for t in range(g + WEIGHT_LAG):
            if t < g:
                sp = _softplus(z_ref[t].astype(BF16))
                c_ref[t] = jnp.dot(suffix, sp, preferred_element_type=F32)
            if t >= WEIGHT_LAG:
                hh = t - WEIGHT_LAG
                z_next = scores(hh, j_next, False)
                w = jnp.exp(z_ref[hh] - c_ref[hh] - runs[hh])
                acc_ref[hh] += jnp.dot(vt_ref[j, hh * HEAD_DIM:(hh + 1) * HEAD_DIM, :],
                                       w.astype(BF16), preferred_element_type=F32)
                run_ref[hh:hh + 1, :] = runs[hh] + c_ref[hh, 0:1, :]
                z_ref[hh] = z_next
        return jnp.min(run_ref[...])

    for hh in range(g):
        z_ref[hh] = scores(hh, i, True)
    acc_ref[...] = jnp.zeros(acc_ref.shape, F32)
    run_ref[...] = jnp.zeros(run_ref.shape, F32)
    lowest = block(i, jnp.maximum(i - 1, 0))

    def cond(state):
        n, lowest = state
        return jnp.logical_and(n < i, lowest < SB_DEAD)

    def body(state):
        n, _ = state
        j = i - 1 - n
        return n + 1, block(j, jnp.maximum(j - 1, 0))

    lax.while_loop(cond, body, (jnp.int32(0), lowest))
    for hh in range(g):
        o_ref[hh * HEAD_DIM:(hh + 1) * HEAD_DIM, :] = acc_ref[hh].astype(o_ref.dtype)


def _fox_kernel(qt_ref, augqt_ref, k_ref, augk_ref, vt_ref, o_ref, rhs_ref, s_ref, acc_ref, m_ref,
                bmax_ref, *, tq):
    g = N_HEADS
    i = pl.program_id(1)
    _fill_query_operand(rhs_ref, qt_ref, augqt_ref, i == 0)
    den_rows = jnp.ones((DEN_ROWS, tq), BF16)

    def scores(hh, j, diagonal):
        start = pl.multiple_of(j * tq, tq)
        pair = slice((hh // 2) * PAIR, (hh // 2 + 1) * PAIR)
        lhs = jnp.concatenate([k_ref[pl.ds(start, tq), pair], augk_ref[pl.ds(start, tq), :]],
                              axis=1)
        s = jnp.dot(lhs, rhs_ref[hh], preferred_element_type=F32)
        if diagonal:
            rows = lax.broadcasted_iota(jnp.int32, (tq, tq), 0)
            cols = lax.broadcasted_iota(jnp.int32, (tq, tq), 1)
            s = jnp.where(rows <= cols, s, NEG_BIG)
        return s

    def keep(hh, s):
        s_ref[hh] = s
        bmax_ref[hh:hh + 1, :] = jnp.max(s, axis=0, keepdims=True)

    def block(j, j_next):
        for hh in range(g):
            s_next = scores(hh, j_next, False)
            m = m_ref[hh:hh + 1, :]
            m_new = jnp.maximum(m, bmax_ref[hh:hh + 1, :])
            alpha = jnp.exp2(m - m_new)
            p = jnp.exp2(s_ref[hh] - m_new).astype(BF16)
            m_ref[hh:hh + 1, :] = m_new
            vt = jnp.concatenate([vt_ref[j, hh * HEAD_DIM:(hh + 1) * HEAD_DIM, :], den_rows], axis=0)
            acc_ref[hh] = alpha * acc_ref[hh] + jnp.dot(vt, p, preferred_element_type=F32)
            keep(hh, s_next)

    for hh in range(g):
        keep(hh, scores(hh, i, True))
    acc_ref[...] = jnp.zeros(acc_ref.shape, F32)
    m_ref[...] = jnp.full(m_ref.shape, NEG_BIG, F32)
    block(i, 0)
    last = jnp.maximum(i - 1, 0)

    def run_blocks(first, count):
        for u in range(count):
            block(first + u, jnp.minimum(first + u + 1, last))

    @pl.loop(0, i // 4)
    def _(n):
        run_blocks(4 * n, 4)

    @pl.when(i % 4 >= 2)
    def _():
        run_blocks(i - i % 4, 2)

    @pl.when(i % 2 == 1)
    def _():
        run_blocks(i - 1, 1)

    for hh in range(g):
        o_ref[hh * HEAD_DIM:(hh + 1) * HEAD_DIM, :] = (
            acc_ref[hh, :HEAD_DIM, :] / acc_ref[hh, HEAD_DIM:HEAD_DIM + 1, :]).astype(o_ref.dtype)


def _attn_call(qt, k, vt, tq, aug=None):
    b, d, s = qt.shape
    nk = s // tq
    q_spec = pl.BlockSpec((None, d, tq), lambda bi, i: (bi, 0, i))
    k_spec = pl.BlockSpec((None, s, d), lambda bi, i: (bi, 0, 0))
    vt_spec = pl.BlockSpec((None, nk, d, tq), lambda bi, i: (bi, 0, 0, 0))
    if aug is None:
        body, name, kdim, acc_rows, n_row_states = _sb_kernel, "sb_attn", PAIR, HEAD_DIM, 1
        extra = [pltpu.VMEM((N_HEADS, tq, tq), F32)]
        operands, in_specs = (qt, k, vt), [q_spec, k_spec, vt_spec]
    else:
        augk, augqt = aug
        body, name, kdim, acc_rows = _fox_kernel, "fox_attn", 2 * PAIR, HEAD_DIM + DEN_ROWS
        n_row_states, extra = 2, []
        operands = (qt, augqt, k, augk, vt)
        in_specs = [q_spec, pl.BlockSpec((None, N_HEADS * AUG, tq), lambda bi, i: (bi, 0, i)),
                    k_spec, pl.BlockSpec((None, s, N_HEADS * AUG), lambda bi, i: (bi, 0, 0)),
                    vt_spec]
    return pl.pallas_call(
        functools.partial(body, tq=tq),
        grid=(b, s // tq),
        in_specs=in_specs,
        out_specs=q_spec,
        out_shape=jax.ShapeDtypeStruct((b, d, s), BF16),
        scratch_shapes=[pltpu.VMEM((N_HEADS, kdim, tq), BF16),
                        pltpu.VMEM((N_HEADS, tq, tq), F32),
                        pltpu.VMEM((N_HEADS, acc_rows, tq), F32)]
        + [pltpu.VMEM((N_HEADS, tq), F32)] * n_row_states + extra,
        compiler_params=pltpu.CompilerParams(
            dimension_semantics=("parallel", "arbitrary"), vmem_limit_bytes=V7X_VMEM_LIMIT),
        name=name,
    )(*operands)


def _post_kernel(x_ref, yst_ref, yft_ref, gate_ref, mod_ref, gffn_ref, gfin_ref, ws_ref, wfx_ref,
                 wo_ref, wg_ref, wu_ref, wd_ref, o_ref, *, bounds):
    d = x_ref.shape[-1]
    mod = mod_ref[...]
    a = jnp.dot(yst_ref[...].T, ws_ref[...], preferred_element_type=F32)
    b = jnp.dot(yft_ref[...].T, wfx_ref[...], preferred_element_type=F32)
    gate = gate_ref[...].astype(F32)
    merged = gate[:, :d] * a + gate[:, d:] * b
    x1 = x_ref[...] + mod[2:3, :] * jnp.dot(merged.astype(BF16), wo_ref[...],
                                            preferred_element_type=F32)
    h2 = _rms_modulate(x1, gffn_ref[...], mod[4:5, :], mod[3:4, :]).astype(BF16)
    ffn = None
    for lo, hi in zip(bounds[:-1], bounds[1:]):
        sl = slice(lo, hi)
        gt = jnp.dot(h2, wg_ref[:, sl], preferred_element_type=F32)
        up = jnp.dot(h2, wu_ref[:, sl], preferred_element_type=F32)
        act = (gt * jax.nn.sigmoid(gt) * up).astype(BF16)
        part = jnp.dot(act, wd_ref[sl, :], preferred_element_type=F32)
        ffn = part if ffn is None else ffn + part
    x2 = x1 + mod[5:6, :] * ffn
    ms = jnp.mean(x2 * x2, axis=-1, keepdims=True)
    o_ref[...] = x2 * lax.rsqrt(ms + RMS_EPS) * gfin_ref[...]


def _post_call(x, yt_sb, yt_fx, gates, mod3, g_ffn, g_final, w_bs, w_bf, w_out, w_gate, w_up,
               w_down, tm):
    b, s, d = x.shape
    row = lambda bi, i: (bi, i, 0)
    yt_spec = pl.BlockSpec((None, D_HEADS, tm), lambda bi, i: (bi, 0, i))
    dff = w_gate.shape[1]
    half = -(-dff // (2 * V7X_MXU_WIDTH)) * V7X_MXU_WIDTH
    bounds = (0, half, dff) if half < dff else (0, dff)
    weights = (w_bs, w_bf, w_out, w_gate, w_up, w_down)
    return pl.pallas_call(
        functools.partial(_post_kernel, bounds=bounds),
        grid=(b, s // tm),
        in_specs=[pl.BlockSpec((None, tm, d), row), yt_spec, yt_spec,
                  pl.BlockSpec((None, tm, 2 * d), row),
                  pl.BlockSpec((None, N_MOD, d), lambda bi, i: (bi, 0, 0)),
                  _const_spec((1, d)), _const_spec((1, d))]
        + [_const_spec(w.shape) for w in weights],
        out_specs=pl.BlockSpec((None, tm, d), row),
        out_shape=jax.ShapeDtypeStruct((b, s, d), F32),
        compiler_params=pltpu.CompilerParams(
            dimension_semantics=("parallel", "parallel"), vmem_limit_bytes=V7X_VMEM_LIMIT),
        name="post",
    )(x, yt_sb, yt_fx, gates, mod3, g_ffn, g_final, *weights)


def kernel(x, c, w_ada, b_ada, g_mix, w_in, b_forget, b_gate, w_branch_sb, w_branch_fox,
           w_out, g_ffn, w_ffn_gate, w_ffn_up, w_ffn_down, g_final):
    b, s, d = x.shape
    depth = w_ada.shape[0]
    assert depth == 1, "the final RMSNorm is fused into the last (only) layer's closing call"
    tm = min(512, s)
    tq = min(256, s)
    dh = D_HEADS
    for l in range(depth):
        mod3 = _mod_call(c, w_ada, b_ada, l).reshape(b, N_MOD, d)
        w = w_in[l]
        w_k = jnp.concatenate([w[:, dh:2 * dh], w[:, 4 * dh:5 * dh]], axis=1).astype(BF16)
        w_qvt = jnp.concatenate([w[:, 0:dh], w[:, 3 * dh:4 * dh], w[:, 2 * dh:3 * dh],
                                 w[:, 5 * dh:6 * dh], w[:, 6 * dh:6 * dh + N_HEADS],
                                 jnp.zeros((d, F_PAD_ROWS - N_HEADS), w.dtype)],
                                axis=1).T.astype(BF16)
        w_g = w[:, 6 * dh + N_HEADS:].astype(BF16)
        k_sb, k_fx, qt_sb, qt_fx, vt_sb, vt_fx, lf_t, gates = _inproj_call(
            x, mod3, g_mix[l].reshape(1, d), w_k, w_qvt, w_g, b_forget[l].reshape(N_HEADS, 1),
            b_gate[l].reshape(1, 2 * d), tm, tq)

        yt_sb = _attn_call(qt_sb, k_sb, vt_sb, tq)
        aug = _cumsum_call(lf_t)
        yt_fx = _attn_call(qt_fx, k_fx, vt_fx, tq, aug)

        x = _post_call(x, yt_sb, yt_fx, gates, mod3, g_ffn[l].reshape(1, d), g_final.reshape(1, d),
                       w_branch_sb[l].astype(BF16), w_branch_fox[l].astype(BF16),
                       w_out[l].astype(BF16), w_ffn_gate[l].astype(BF16),
                       w_ffn_up[l].astype(BF16), w_ffn_down[l].astype(BF16), tm)
    return x
```

```python
import functools
import math

import jax
import jax.numpy as jnp
from jax import lax
from jax.experimental import pallas as pl
from jax.experimental.pallas import tpu as pltpu

F32 = jnp.float32
BF16 = jnp.bfloat16

HEAD_DIM = 64
N_HEADS = 8
D_HEADS = N_HEADS * HEAD_DIM
PAIR = 2 * HEAD_DIM
N_MOD = 6
RMS_EPS = 1e-6
LOG2E = math.log2(math.e)
SB_Q_SCALE = HEAD_DIM ** -0.5
FOX_Q_SCALE = LOG2E * HEAD_DIM ** -0.5
AUG = 16
DEN_ROWS = 16
WEIGHT_LAG = 3
F_PAD_ROWS = 16
NEG_BIG = -1e30
SB_DEAD = 160.0 * math.log(2.0)
V7X_VMEM_LIMIT = 56 * 1024 * 1024
V7X_MXU_WIDTH = 256
NT_DIMS = (((1,), (1,)), ((), ()))


def _rms_modulate(x, g, scale, shift):
    ms = jnp.mean(x * x, axis=-1, keepdims=True)
    y = x * lax.rsqrt(ms + RMS_EPS) * g
    return y * (1.0 + scale) + shift


def _const_spec(shape):
    return pl.BlockSpec(shape, lambda *_: (0,) * len(shape), pipeline_mode=pl.Buffered(1))


def _mod_kernel(c_ref, w_ref, b_ref, o_ref):
    c = c_ref[...]
    ca = c * jax.nn.sigmoid(c)
    o_ref[...] = jnp.dot(ca, w_ref[...], preferred_element_type=F32,
                         precision=lax.Precision.HIGHEST) + b_ref[...]


def _mod_call(c, w_ada, b_ada, layer):
    b, d = c.shape
    depth, _, n = w_ada.shape
    tn = 1024
    return pl.pallas_call(
        _mod_kernel,
        grid=(n // tn,),
        in_specs=[pl.BlockSpec((b, d), lambda j: (0, 0)),
                  pl.BlockSpec((None, d, tn), lambda j: (layer, 0, j)),
                  pl.BlockSpec((None, 1, tn), lambda j: (layer, 0, j))],
        out_specs=pl.BlockSpec((b, tn), lambda j: (0, j)),
        out_shape=jax.ShapeDtypeStruct((b, n), F32),
        name="mod",
    )(c, w_ada, b_ada.reshape(depth, 1, n))


def _inproj_kernel(x_ref, mod_ref, g_ref, wk_ref, wqvt_ref, wg_ref, bf_ref, bg_ref,
                   ks_ref, kf_ref, qts_ref, qtf_ref, vts_ref, vtf_ref, lft_ref, gate_ref, *, tk):
    mod = mod_ref[...]
    h = _rms_modulate(x_ref[...], g_ref[...], mod[1:2, :], mod[0:1, :]).astype(BF16)
    d = D_HEADS
    kk = jnp.dot(h, wk_ref[...], preferred_element_type=F32)
    ks_ref[...] = kk[:, :d].astype(BF16)
    kf_ref[...] = kk[:, d:].astype(BF16)
    t = lax.dot_general(wqvt_ref[...], h, NT_DIMS, preferred_element_type=F32)
    qts_ref[...] = (t[0 * d:1 * d] * SB_Q_SCALE).astype(BF16)
    qtf_ref[...] = (t[1 * d:2 * d] * FOX_Q_SCALE).astype(BF16)
    for c in range(vts_ref.shape[0]):
        vts_ref[c] = t[2 * d:3 * d, c * tk:(c + 1) * tk].astype(BF16)
        vtf_ref[c] = t[3 * d:4 * d, c * tk:(c + 1) * tk].astype(BF16)
    f = t[4 * d:4 * d + N_HEADS] + bf_ref[...]
    lft_ref[...] = -(jnp.maximum(-f, 0.0) + jnp.log(1.0 + jnp.exp(-jnp.abs(f)))) * LOG2E
    gl = jnp.dot(h, wg_ref[...], preferred_element_type=F32) + bg_ref[...]
    gate_ref[...] = jax.nn.sigmoid(gl).astype(BF16)


def _inproj_call(x, mod3, g_mix, w_k, w_qvt, w_g, b_f, b_g, tm, tk):
    b, s, d = x.shape
    row = lambda bi, i: (bi, i, 0)
    col = lambda bi, i: (bi, 0, i)
    k_spec = pl.BlockSpec((None, tm, D_HEADS), row)
    qt_spec = pl.BlockSpec((None, D_HEADS, tm), col)
    vt_spec = pl.BlockSpec((None, tm // tk, D_HEADS, tk), lambda bi, i: (bi, i, 0, 0))
    return pl.pallas_call(
        functools.partial(_inproj_kernel, tk=tk),
        grid=(b, s // tm),
        in_specs=[pl.BlockSpec((None, tm, d), row),
                  pl.BlockSpec((None, N_MOD, d), lambda bi, i: (bi, 0, 0)),
                  _const_spec((1, d)),
                  _const_spec(w_k.shape), _const_spec(w_qvt.shape),
                  _const_spec(w_g.shape), _const_spec(b_f.shape), _const_spec(b_g.shape)],
        out_specs=[k_spec, k_spec, qt_spec, qt_spec, vt_spec, vt_spec,
                   pl.BlockSpec((None, N_HEADS, tm), col),
                   pl.BlockSpec((None, tm, 2 * d), row)],
        out_shape=[jax.ShapeDtypeStruct((b, s, D_HEADS), BF16)] * 2
        + [jax.ShapeDtypeStruct((b, D_HEADS, s), BF16)] * 2
        + [jax.ShapeDtypeStruct((b, s // tk, D_HEADS, tk), BF16)] * 2
        + [jax.ShapeDtypeStruct((b, N_HEADS, s), F32),
           jax.ShapeDtypeStruct((b, s, 2 * d), BF16)],
        compiler_params=pltpu.CompilerParams(
            dimension_semantics=("parallel", "parallel"), vmem_limit_bytes=V7X_VMEM_LIMIT),
        name="inproj",
    )(x, mod3, g_mix, w_k, w_qvt, w_g, b_f, b_g)


def _split3(v):
    hi = v.astype(BF16).astype(F32)
    r = v - hi
    mid = r.astype(BF16).astype(F32)
    lo = (r - mid).astype(BF16).astype(F32)
    return hi, mid, lo


def _cumsum_kernel(lft_ref, augk_ref, augqt_ref, *, chunk):
    h, s = lft_ref.shape
    n3 = 3 * h
    r = lax.broadcasted_iota(jnp.int32, (chunk, chunk), 0)
    c = lax.broadcasted_iota(jnp.int32, (chunk, chunk), 1)
    upper = (r <= c).astype(BF16)

    qr = lax.broadcasted_iota(jnp.int32, (h * AUG, n3), 0)
    qc = lax.broadcasted_iota(jnp.int32, (h * AUG, n3), 1)
    qrow = lax.broadcasted_iota(jnp.int32, (h * AUG, 1), 0)
    place_q = jnp.zeros((h * AUG, n3), F32)
    place_k = jnp.zeros((h * AUG, n3), F32)
    ones_q = jnp.zeros((h * AUG, 1), F32)
    ones_k = jnp.zeros((h * AUG, 1), F32)
    for hh in range(h):
        for p in range(3):
            place_q = jnp.where((qr == hh * AUG + 3 + p) & (qc == p * h + hh), 1.0, place_q)
            place_k = jnp.where((qr == hh * AUG + p) & (qc == p * h + hh), -1.0, place_k)
            ones_q = jnp.where(qrow == hh * AUG + p, 1.0, ones_q)
            ones_k = jnp.where(qrow == hh * AUG + 3 + p, 1.0, ones_k)
    place_q = place_q.astype(BF16)
    place_k = place_k.astype(BF16)

    carry = jnp.zeros((h, 1), F32)
    for ci in range(s // chunk):
        sl = slice(ci * chunk, (ci + 1) * chunk)
        pieces = jnp.concatenate(_split3(lft_ref[:, sl]), axis=0).astype(BF16)
        p3 = jnp.dot(pieces, upper, preferred_element_type=F32)
        cum = (p3[0:h] + p3[h:2 * h]) + p3[2 * h:n3] + carry
        carry = cum[:, chunk - 1:chunk]
        stacked = jnp.concatenate(_split3(cum), axis=0).astype(BF16)
        augqt_ref[:, sl] = (jnp.dot(place_q, stacked, preferred_element_type=F32)
                            + ones_q).astype(BF16)
        augkt = jnp.dot(place_k, stacked, preferred_element_type=F32) + ones_k
        augk_ref[sl, :] = augkt.T.astype(BF16)


def _cumsum_call(lf_t):
    b, h, s = lf_t.shape
    return pl.pallas_call(
        functools.partial(_cumsum_kernel, chunk=min(256, s)),
        grid=(b,),
        in_specs=[pl.BlockSpec((None, h, s), lambda bi: (bi, 0, 0))],
        out_specs=[pl.BlockSpec((None, s, h * AUG), lambda bi: (bi, 0, 0)),
                   pl.BlockSpec((None, h * AUG, s), lambda bi: (bi, 0, 0))],
        out_shape=[jax.ShapeDtypeStruct((b, s, h * AUG), BF16),
                   jax.ShapeDtypeStruct((b, h * AUG, s), BF16)],
        compiler_params=pltpu.CompilerParams(dimension_semantics=("parallel",)),
        name="cumsum",
    )(lf_t)


def _softplus(z):
    return jnp.maximum(z, 0.0) + jnp.log(1.0 + jnp.exp(-jnp.abs(z)))


def _fill_query_operand(rhs_ref, qt_ref, augqt_ref, first_tile):
    @pl.when(first_tile)
    def _():
        rhs_ref[...] = jnp.zeros(rhs_ref.shape, rhs_ref.dtype)

    for hh in range(N_HEADS):
        half = hh % 2
        rhs_ref[hh, half * HEAD_DIM:(half + 1) * HEAD_DIM, :] = (
            qt_ref[hh * HEAD_DIM:(hh + 1) * HEAD_DIM, :])
        if augqt_ref is not None:
            rhs_ref[hh, PAIR + hh * AUG:PAIR + (hh + 1) * AUG, :] = (
                augqt_ref[hh * AUG:(hh + 1) * AUG, :])


def _sb_kernel(qt_ref, k_ref, vt_ref, o_ref, rhs_ref, z_ref, acc_ref, run_ref, c_ref, *, tq):
    g = N_HEADS
    i = pl.program_id(1)
    _fill_query_operand(rhs_ref, qt_ref, None, i == 0)
    rows = lax.broadcasted_iota(jnp.int32, (tq, tq), 0)
    cols = lax.broadcasted_iota(jnp.int32, (tq, tq), 1)
    suffix = (cols >= rows).astype(BF16)

    def scores(hh, j, diagonal):
        start = pl.multiple_of(j * tq, tq)
        z = jnp.dot(k_ref[pl.ds(start, tq), (hh // 2) * PAIR:(hh // 2 + 1) * PAIR], rhs_ref[hh],
                    preferred_element_type=F32)
        return jnp.where(rows < cols, z, NEG_BIG) if diagonal else z

    def block(j, j_next):
        runs = [run_ref[hh:hh + 1, :] for hh in range(g)]
        for t in range(g + WEIGHT_LAG):
            if t < g:
                sp = _softplus(z_ref[t].astype(BF16))
                c_ref[t] = jnp.dot(suffix, sp, preferred_element_type=F32)
            if t >= WEIGHT_LAG:
                hh = t - WEIGHT_LAG
                z_next = scores(hh, j_next, False)
                w = jnp.exp(z_ref[hh] - c_ref[hh] - runs[hh])
                acc_ref[hh] += jnp.dot(vt_ref[j, hh * HEAD_DIM:(hh + 1) * HEAD_DIM, :],
                                       w.astype(BF16), preferred_element_type=F32)
                run_ref[hh:hh + 1, :] = runs[hh] + c_ref[hh, 0:1, :]
                z_ref[hh] = z_next
        return jnp.min(run_ref[...])

    def start(with_previous_block):
        for hh in range(g):
            z_ref[hh] = scores(hh, i, True)
        acc_ref[...] = jnp.zeros(acc_ref.shape, F32)
        run_ref[...] = jnp.zeros(run_ref.shape, F32)
        block(i, jnp.maximum(i - 1, 0))
        if with_previous_block:
            block(i - 1, jnp.maximum(i - 2, 0))

    @pl.when(i == 0)
    def _():
        start(False)

    @pl.when(i > 0)
    def _():
        start(True)

    def cond(state):
        n, lowest = state
        return jnp.logical_and(n < i, lowest < SB_DEAD)

    def body(state):
        n, _ = state
        j = i - 1 - n
        return n + 1, block(j, jnp.maximum(j - 1, 0))

    lax.while_loop(cond, body, (jnp.minimum(i, 1), jnp.min(run_ref[...])))
    for hh in range(g):
        o_ref[hh * HEAD_DIM:(hh + 1) * HEAD_DIM, :] = acc_ref[hh].astype(o_ref.dtype)


def _fox_kernel(qt_ref, augqt_ref, k_ref, augk_ref, vt_ref, o_ref, rhs_ref, s_ref, acc_ref, m_ref,
                bmax_ref, *, tq):
    g = N_HEADS
    i = pl.program_id(1)
    _fill_query_operand(rhs_ref, qt_ref, augqt_ref, i == 0)
    den_rows = jnp.ones((DEN_ROWS, tq), BF16)

    def scores(hh, j, diagonal):
        start = pl.multiple_of(j * tq, tq)
        pair = slice((hh // 2) * PAIR, (hh // 2 + 1) * PAIR)
        lhs = jnp.concatenate([k_ref[pl.ds(start, tq), pair], augk_ref[pl.ds(start, tq), :]],
                              axis=1)
        s = jnp.dot(lhs, rhs_ref[hh], preferred_element_type=F32)
        if diagonal:
            rows = lax.broadcasted_iota(jnp.int32, (tq, tq), 0)
            cols = lax.broadcasted_iota(jnp.int32, (tq, tq), 1)
            s = jnp.where(rows <= cols, s, NEG_BIG)
        return s

    def keep(hh, s):
        s_ref[hh] = s
        bmax_ref[hh:hh + 1, :] = jnp.max(s, axis=0, keepdims=True)

    def block(j, j_next):
        for hh in range(g):
            s_next = scores(hh, j_next, False)
            m = m_ref[hh:hh + 1, :]
            m_new = jnp.maximum(m, bmax_ref[hh:hh + 1, :])
            alpha = jnp.exp2(m - m_new)
            p = jnp.exp2(s_ref[hh] - m_new).astype(BF16)
            m_ref[hh:hh + 1, :] = m_new
            vt = jnp.concatenate([vt_ref[j, hh * HEAD_DIM:(hh + 1) * HEAD_DIM, :], den_rows], axis=0)
            acc_ref[hh] = alpha * acc_ref[hh] + jnp.dot(vt, p, preferred_element_type=F32)
            keep(hh, s_next)

    for hh in range(g):
        keep(hh, scores(hh, i, True))
    acc_ref[...] = jnp.zeros(acc_ref.shape, F32)
    m_ref[...] = jnp.full(m_ref.shape, NEG_BIG, F32)
    block(i, 0)
    last = jnp.maximum(i - 1, 0)

    def run_blocks(first, count):
        for u in range(count):
            block(first + u, jnp.minimum(first + u + 1, last))

    @pl.loop(0, i // 4)
    def _(n):
        run_blocks(4 * n, 4)

    @pl.when(i % 4 >= 2)
    def _():
        run_blocks(i - i % 4, 2)

    @pl.when(i % 2 == 1)
    def _():
        run_blocks(i - 1, 1)

    for hh in range(g):
        o_ref[hh * HEAD_DIM:(hh + 1) * HEAD_DIM, :] = (
            acc_ref[hh, :HEAD_DIM, :] / acc_ref[hh, HEAD_DIM:HEAD_DIM + 1, :]).astype(o_ref.dtype)


def _attn_call(qt, k, vt, tq, aug=None):
    b, d, s = qt.shape
    nk = s // tq
    q_spec = pl.BlockSpec((None, d, tq), lambda bi, i: (bi, 0, i))
    k_spec = pl.BlockSpec((None, s, d), lambda bi, i: (bi, 0, 0))
    vt_spec = pl.BlockSpec((None, nk, d, tq), lambda bi, i: (bi, 0, 0, 0))
    if aug is None:
        body, name, kdim, acc_rows, n_row_states = _sb_kernel, "sb_attn", PAIR, HEAD_DIM, 1
        extra = [pltpu.VMEM((N_HEADS, tq, tq), F32)]
        operands, in_specs = (qt, k, vt), [q_spec, k_spec, vt_spec]
    else:
        augk, augqt = aug
        body, name, kdim, acc_rows = _fox_kernel, "fox_attn", 2 * PAIR, HEAD_DIM + DEN_ROWS
        n_row_states, extra = 2, []
        operands = (qt, augqt, k, augk, vt)
        in_specs = [q_spec, pl.BlockSpec((None, N_HEADS * AUG, tq), lambda bi, i: (bi, 0, i)),
                    k_spec, pl.BlockSpec((None, s, N_HEADS * AUG), lambda bi, i: (bi, 0, 0)),
                    vt_spec]
    return pl.pallas_call(
        functools.partial(body, tq=tq),
        grid=(b, s // tq),
        in_specs=in_specs,
        out_specs=q_spec,
        out_shape=jax.ShapeDtypeStruct((b, d, s), BF16),
        scratch_shapes=[pltpu.VMEM((N_HEADS, kdim, tq), BF16),
                        pltpu.VMEM((N_HEADS, tq, tq), F32),
                        pltpu.VMEM((N_HEADS, acc_rows, tq), F32)]
        + [pltpu.VMEM((N_HEADS, tq), F32)] * n_row_states + extra,
        compiler_params=pltpu.CompilerParams(
            dimension_semantics=("parallel", "arbitrary"), vmem_limit_bytes=V7X_VMEM_LIMIT),
        name=name,
    )(*operands)


def _post_kernel(x_ref, yst_ref, yft_ref, gate_ref, mod_ref, gffn_ref, gfin_ref, ws_ref, wfx_ref,
                 wo_ref, wg_ref, wu_ref, wd_ref, o_ref, *, bounds):
    d = x_ref.shape[-1]
    mod = mod_ref[...]
    a = jnp.dot(yst_ref[...].T, ws_ref[...], preferred_element_type=F32)
    b = jnp.dot(yft_ref[...].T, wfx_ref[...], preferred_element_type=F32)
    gate = gate_ref[...].astype(F32)
    merged = gate[:, :d] * a + gate[:, d:] * b
    x1 = x_ref[...] + mod[2:3, :] * jnp.dot(merged.astype(BF16), wo_ref[...],
                                            preferred_element_type=F32)
    h2 = _rms_modulate(x1, gffn_ref[...], mod[4:5, :], mod[3:4, :]).astype(BF16)
    ffn = None
    for lo, hi in zip(bounds[:-1], bounds[1:]):
        sl = slice(lo, hi)
        gt = jnp.dot(h2, wg_ref[:, sl], preferred_element_type=F32)
        up = jnp.dot(h2, wu_ref[:, sl], preferred_element_type=F32)
        act = (gt * jax.nn.sigmoid(gt) * up).astype(BF16)
        part = jnp.dot(act, wd_ref[sl, :], preferred_element_type=F32)
        ffn = part if ffn is None else ffn + part
    x2 = x1 + mod[5:6, :] * ffn
    ms = jnp.mean(x2 * x2, axis=-1, keepdims=True)
    o_ref[...] = x2 * lax.rsqrt(ms + RMS_EPS) * gfin_ref[...]


def _post_call(x, yt_sb, yt_fx, gates, mod3, g_ffn, g_final, w_bs, w_bf, w_out, w_gate, w_up,
               w_down, tm):
    b, s, d = x.shape
    row = lambda bi, i: (bi, i, 0)
    yt_spec = pl.BlockSpec((None, D_HEADS, tm), lambda bi, i: (bi, 0, i))
    dff = w_gate.shape[1]
    half = -(-dff // (2 * V7X_MXU_WIDTH)) * V7X_MXU_WIDTH
    bounds = (0, half, dff) if half < dff else (0, dff)
    weights = (w_bs, w_bf, w_out, w_gate, w_up, w_down)
    return pl.pallas_call(
        functools.partial(_post_kernel, bounds=bounds),
        grid=(b, s // tm),
        in_specs=[pl.BlockSpec((None, tm, d), row), yt_spec, yt_spec,
                  pl.BlockSpec((None, tm, 2 * d), row),
                  pl.BlockSpec((None, N_MOD, d), lambda bi, i: (bi, 0, 0)),
                  _const_spec((1, d)), _const_spec((1, d))]
        + [_const_spec(w.shape) for w in weights],
        out_specs=pl.BlockSpec((None, tm, d), row),
        out_shape=jax.ShapeDtypeStruct((b, s, d), F32),
        compiler_params=pltpu.CompilerParams(
            dimension_semantics=("parallel", "parallel"), vmem_limit_bytes=V7X_VMEM_LIMIT),
        name="post",
    )(x, yt_sb, yt_fx, gates, mod3, g_ffn, g_final, *weights)


def kernel(x, c, w_ada, b_ada, g_mix, w_in, b_forget, b_gate, w_branch_sb, w_branch_fox,
           w_out, g_ffn, w_ffn_gate, w_ffn_up, w_ffn_down, g_final):
    b, s, d = x.shape
    depth = w_ada.shape[0]
    assert depth == 1, "the final RMSNorm is fused into the last (only) layer's closing call"
    tm = min(512, s)
    tq = min(256, s)
    dh = D_HEADS
    for l in range(depth):
        mod3 = _mod_call(c, w_ada, b_ada, l).reshape(b, N_MOD, d)
        w = w_in[l]
        w_k = jnp.concatenate([w[:, dh:2 * dh], w[:, 4 * dh:5 * dh]], axis=1).astype(BF16)
        w_qvt = jnp.concatenate([w[:, 0:dh], w[:, 3 * dh:4 * dh], w[:, 2 * dh:3 * dh],
                                 w[:, 5 * dh:6 * dh], w[:, 6 * dh:6 * dh + N_HEADS],
                                 jnp.zeros((d, F_PAD_ROWS - N_HEADS), w.dtype)],
                                axis=1).T.astype(BF16)
        w_g = w[:, 6 * dh + N_HEADS:].astype(BF16)
        k_sb, k_fx, qt_sb, qt_fx, vt_sb, vt_fx, lf_t, gates = _inproj_call(
            x, mod3, g_mix[l].reshape(1, d), w_k, w_qvt, w_g, b_forget[l].reshape(N_HEADS, 1),
            b_gate[l].reshape(1, 2 * d), tm, tq)

        yt_sb = _attn_call(qt_sb, k_sb, vt_sb, tq)
        aug = _cumsum_call(lf_t)
        yt_fx = _attn_call(qt_fx, k_fx, vt_fx, tq, aug)

        x = _post_call(x, yt_sb, yt_fx, gates, mod3, g_ffn[l].reshape(1, d), g_final.reshape(1, d),
                       w_branch_sb[l].astype(BF16), w_branch_fox[l].astype(BF16),
                       w_out[l].astype(BF16), w_ffn_gate[l].astype(BF16),
                       w_ffn_up[l].astype(BF16), w_ffn_down[l].astype(BF16), tm)
    return x
```

```python
import functools
import math

import jax
import jax.numpy as jnp
from jax import lax
from jax.experimental import pallas as pl
from jax.experimental.pallas import tpu as pltpu

F32 = jnp.float32
BF16 = jnp.bfloat16

HEAD_DIM = 64
N_HEADS = 8
D_HEADS = N_HEADS * HEAD_DIM
PAIR = 2 * HEAD_DIM
N_MOD = 6
RMS_EPS = 1e-6
LOG2E = math.log2(math.e)
SB_Q_SCALE = HEAD_DIM ** -0.5
FOX_Q_SCALE = LOG2E * HEAD_DIM ** -0.5
AUG = 16
DEN_ROWS = 16
WEIGHT_LAG = 3
ATTN_SUBTILES = 2
F_PAD_ROWS = 16
NEG_BIG = -1e30
SB_DEAD = 160.0 * math.log(2.0)
V7X_VMEM_LIMIT = 56 * 1024 * 1024
V7X_MXU_WIDTH = 256
NT_DIMS = (((1,), (1,)), ((), ()))


def _rms_modulate(x, g, scale, shift):
    ms = jnp.mean(x * x, axis=-1, keepdims=True)
    y = x * lax.rsqrt(ms + RMS_EPS) * g
    return y * (1.0 + scale) + shift


def _const_spec(shape):
    return pl.BlockSpec(shape, lambda *_: (0,) * len(shape), pipeline_mode=pl.Buffered(1))


def _mod_kernel(c_ref, w_ref, b_ref, o_ref):
    c = c_ref[...]
    ca = c * jax.nn.sigmoid(c)
    o_ref[...] = jnp.dot(ca, w_ref[...], preferred_element_type=F32,
                         precision=lax.Precision.HIGHEST) + b_ref[...]


def _mod_call(c, w_ada, b_ada, layer):
    b, d = c.shape
    depth, _, n = w_ada.shape
    tn = 1024
    return pl.pallas_call(
        _mod_kernel,
        grid=(n // tn,),
        in_specs=[pl.BlockSpec((b, d), lambda j: (0, 0)),
                  pl.BlockSpec((None, d, tn), lambda j: (layer, 0, j)),
                  pl.BlockSpec((None, 1, tn), lambda j: (layer, 0, j))],
        out_specs=pl.BlockSpec((b, tn), lambda j: (0, j)),
        out_shape=jax.ShapeDtypeStruct((b, n), F32),
        name="mod",
    )(c, w_ada, b_ada.reshape(depth, 1, n))


def _inproj_kernel(x_ref, mod_ref, g_ref, wk_ref, wqvt_ref, wg_ref, bf_ref, bg_ref,
                   ks_ref, kf_ref, qts_ref, qtf_ref, vts_ref, vtf_ref, lft_ref, gate_ref, *, tk):
    mod = mod_ref[...]
    h = _rms_modulate(x_ref[...], g_ref[...], mod[1:2, :], mod[0:1, :]).astype(BF16)
    d = D_HEADS
    kk = jnp.dot(h, wk_ref[...], preferred_element_type=F32)
    ks_ref[...] = kk[:, :d].astype(BF16)
    kf_ref[...] = kk[:, d:].astype(BF16)
    t = lax.dot_general(wqvt_ref[...], h, NT_DIMS, preferred_element_type=F32)
    qts_ref[...] = (t[0 * d:1 * d] * SB_Q_SCALE).astype(BF16)
    qtf_ref[...] = (t[1 * d:2 * d] * FOX_Q_SCALE).astype(BF16)
    for c in range(vts_ref.shape[0]):
        vts_ref[c] = t[2 * d:3 * d, c * tk:(c + 1) * tk].astype(BF16)
        vtf_ref[c] = t[3 * d:4 * d, c * tk:(c + 1) * tk].astype(BF16)
    f = t[4 * d:4 * d + N_HEADS] + bf_ref[...]
    lft_ref[...] = -(jnp.maximum(-f, 0.0) + jnp.log(1.0 + jnp.exp(-jnp.abs(f)))) * LOG2E
    gl = jnp.dot(h, wg_ref[...], preferred_element_type=F32) + bg_ref[...]
    gate_ref[...] = jax.nn.sigmoid(gl).astype(BF16)


def _inproj_call(x, mod3, g_mix, w_k, w_qvt, w_g, b_f, b_g, tm, tk):
    b, s, d = x.shape
    row = lambda bi, i: (bi, i, 0)
    col = lambda bi, i: (bi, 0, i)
    k_spec = pl.BlockSpec((None, tm, D_HEADS), row)
    qt_spec = pl.BlockSpec((None, D_HEADS, tm), col)
    vt_spec = pl.BlockSpec((None, tm // tk, D_HEADS, tk), lambda bi, i: (bi, i, 0, 0))
    return pl.pallas_call(
        functools.partial(_inproj_kernel, tk=tk),
        grid=(b, s // tm),
        in_specs=[pl.BlockSpec((None, tm, d), row),
                  pl.BlockSpec((None, N_MOD, d), lambda bi, i: (bi, 0, 0)),
                  _const_spec((1, d)),
                  _const_spec(w_k.shape), _const_spec(w_qvt.shape),
                  _const_spec(w_g.shape), _const_spec(b_f.shape), _const_spec(b_g.shape)],
        out_specs=[k_spec, k_spec, qt_spec, qt_spec, vt_spec, vt_spec,
                   pl.BlockSpec((None, N_HEADS, tm), col),
                   pl.BlockSpec((None, tm, 2 * d), row)],
        out_shape=[jax.ShapeDtypeStruct((b, s, D_HEADS), BF16)] * 2
        + [jax.ShapeDtypeStruct((b, D_HEADS, s), BF16)] * 2
        + [jax.ShapeDtypeStruct((b, s // tk, D_HEADS, tk), BF16)] * 2
        + [jax.ShapeDtypeStruct((b, N_HEADS, s), F32),
           jax.ShapeDtypeStruct((b, s, 2 * d), BF16)],
        compiler_params=pltpu.CompilerParams(
            dimension_semantics=("parallel", "parallel"), vmem_limit_bytes=V7X_VMEM_LIMIT),
        name="inproj",
    )(x, mod3, g_mix, w_k, w_qvt, w_g, b_f, b_g)


def _split3(v):
    hi = v.astype(BF16).astype(F32)
    r = v - hi
    mid = r.astype(BF16).astype(F32)
    lo = (r - mid).astype(BF16).astype(F32)
    return hi, mid, lo


def _cumsum_kernel(lft_ref, augk_ref, augqt_ref, *, chunk):
    h, s = lft_ref.shape
    n3 = 3 * h
    r = lax.broadcasted_iota(jnp.int32, (chunk, chunk), 0)
    c = lax.broadcasted_iota(jnp.int32, (chunk, chunk), 1)
    upper = (r <= c).astype(BF16)

    qr = lax.broadcasted_iota(jnp.int32, (h * AUG, n3), 0)
    qc = lax.broadcasted_iota(jnp.int32, (h * AUG, n3), 1)
    qrow = lax.broadcasted_iota(jnp.int32, (h * AUG, 1), 0)
    place_q = jnp.zeros((h * AUG, n3), F32)
    place_k = jnp.zeros((h * AUG, n3), F32)
    ones_q = jnp.zeros((h * AUG, 1), F32)
    ones_k = jnp.zeros((h * AUG, 1), F32)
    for hh in range(h):
        for p in range(3):
            place_q = jnp.where((qr == hh * AUG + 3 + p) & (qc == p * h + hh), 1.0, place_q)
            place_k = jnp.where((qr == hh * AUG + p) & (qc == p * h + hh), -1.0, place_k)
            ones_q = jnp.where(qrow == hh * AUG + p, 1.0, ones_q)
            ones_k = jnp.where(qrow == hh * AUG + 3 + p, 1.0, ones_k)
    place_q = place_q.astype(BF16)
    place_k = place_k.astype(BF16)

    carry = jnp.zeros((h, 1), F32)
    for ci in range(s // chunk):
        sl = slice(ci * chunk, (ci + 1) * chunk)
        pieces = jnp.concatenate(_split3(lft_ref[:, sl]), axis=0).astype(BF16)
        p3 = jnp.dot(pieces, upper, preferred_element_type=F32)
        cum = (p3[0:h] + p3[h:2 * h]) + p3[2 * h:n3] + carry
        carry = cum[:, chunk - 1:chunk]
        stacked = jnp.concatenate(_split3(cum), axis=0).astype(BF16)
        augqt_ref[:, sl] = (jnp.dot(place_q, stacked, preferred_element_type=F32)
                            + ones_q).astype(BF16)
        augkt = jnp.dot(place_k, stacked, preferred_element_type=F32) + ones_k
        augk_ref[sl, :] = augkt.T.astype(BF16)


def _cumsum_call(lf_t):
    b, h, s = lf_t.shape
    return pl.pallas_call(
        functools.partial(_cumsum_kernel, chunk=min(256, s)),
        grid=(b,),
        in_specs=[pl.BlockSpec((None, h, s), lambda bi: (bi, 0, 0))],
        out_specs=[pl.BlockSpec((None, s, h * AUG), lambda bi: (bi, 0, 0)),
                   pl.BlockSpec((None, h * AUG, s), lambda bi: (bi, 0, 0))],
        out_shape=[jax.ShapeDtypeStruct((b, s, h * AUG), BF16),
                   jax.ShapeDtypeStruct((b, h * AUG, s), BF16)],
        compiler_params=pltpu.CompilerParams(dimension_semantics=("parallel",)),
        name="cumsum",
    )(lf_t)


def _softplus(z):
    return jnp.maximum(z, 0.0) + jnp.log(1.0 + jnp.exp(-jnp.abs(z)))


def _fill_query_operand(rhs_ref, qt_ref, augqt_ref, first_tile, lanes):
    @pl.when(first_tile)
    def _():
        rhs_ref[...] = jnp.zeros(rhs_ref.shape, rhs_ref.dtype)

    for hh in range(N_HEADS):
        half = hh % 2
        rhs_ref[hh, half * HEAD_DIM:(half + 1) * HEAD_DIM, :] = (
            qt_ref[hh * HEAD_DIM:(hh + 1) * HEAD_DIM, lanes])
        if augqt_ref is not None:
            rhs_ref[hh, PAIR + hh * AUG:PAIR + (hh + 1) * AUG, :] = (
                augqt_ref[hh * AUG:(hh + 1) * AUG, lanes])


def _sb_kernel(*refs, tq, subtiles):
    for sub in range(subtiles):
        _sb_tile(pl.program_id(1) * subtiles + sub, slice(sub * tq, (sub + 1) * tq), *refs, tq=tq)


def _sb_tile(i, lanes, qt_ref, k_ref, vt_ref, o_ref, rhs_ref, z_ref, acc_ref, run_ref, c_ref, *, tq):
    g = N_HEADS
    _fill_query_operand(rhs_ref, qt_ref, None, i == 0, lanes)
    rows = lax.broadcasted_iota(jnp.int32, (tq, tq), 0)
    cols = lax.broadcasted_iota(jnp.int32, (tq, tq), 1)
    suffix = (cols >= rows).astype(BF16)

    def scores(hh, j, diagonal):
        start = pl.multiple_of(j * tq, tq)
        z = jnp.dot(k_ref[pl.ds(start, tq), (hh // 2) * PAIR:(hh // 2 + 1) * PAIR], rhs_ref[hh],
                    preferred_element_type=F32)
        return jnp.where(rows < cols, z, NEG_BIG) if diagonal else z

    def block(j, j_next):
        runs = [run_ref[hh:hh + 1, :] for hh in range(g)]
        for t in range(g + WEIGHT_LAG):
            if t < g:
                sp = _softplus(z_ref[t].astype(BF16))
                c_ref[t] = jnp.dot(suffix, sp, preferred_element_type=F32)
            if t >= WEIGHT_LAG:
                hh = t - WEIGHT_LAG
                z_next = scores(hh, j_next, False)
                w = jnp.exp(z_ref[hh] - c_ref[hh] - runs[hh])
                acc_ref[hh] += jnp.dot(vt_ref[j, hh * HEAD_DIM:(hh + 1) * HEAD_DIM, :],
                                       w.astype(BF16), preferred_element_type=F32)
                run_ref[hh:hh + 1, :] = runs[hh] + c_ref[hh, 0:1, :]
                z_ref[hh] = z_next
        return jnp.min(run_ref[...])

    for hh in range(g):
        z_ref[hh] = scores(hh, i, True)
    acc_ref[...] = jnp.zeros(acc_ref.shape, F32)
    run_ref[...] = jnp.zeros(run_ref.shape, F32)
    lowest = block(i, jnp.maximum(i - 1, 0))

    def cond(state):
        n, lowest = state
        return jnp.logical_and(n < i, lowest < SB_DEAD)

    def body(state):
        n, _ = state
        j = i - 1 - n
        return n + 1, block(j, jnp.maximum(j - 1, 0))

    lax.while_loop(cond, body, (jnp.int32(0), lowest))
    for hh in range(g):
        o_ref[hh * HEAD_DIM:(hh + 1) * HEAD_DIM, lanes] = acc_ref[hh].astype(o_ref.dtype)


def _fox_kernel(*refs, tq, subtiles):
    for sub in range(subtiles):
        _fox_tile(pl.program_id(1) * subtiles + sub, slice(sub * tq, (sub + 1) * tq), *refs, tq=tq)


def _fox_tile(i, lanes, qt_ref, augqt_ref, k_ref, augk_ref, vt_ref, o_ref, rhs_ref, s_ref, acc_ref,
              m_ref, bmax_ref, *, tq):
    g = N_HEADS
    _fill_query_operand(rhs_ref, qt_ref, augqt_ref, i == 0, lanes)
    den_rows = jnp.ones((DEN_ROWS, tq), BF16)

    def scores(hh, j, diagonal):
        start = pl.multiple_of(j * tq, tq)
        pair = slice((hh // 2) * PAIR, (hh // 2 + 1) * PAIR)
        lhs = jnp.concatenate([k_ref[pl.ds(start, tq), pair], augk_ref[pl.ds(start, tq), :]],
                              axis=1)
        s = jnp.dot(lhs, rhs_ref[hh], preferred_element_type=F32)
        if diagonal:
            rows = lax.broadcasted_iota(jnp.int32, (tq, tq), 0)
            cols = lax.broadcasted_iota(jnp.int32, (tq, tq), 1)
            s = jnp.where(rows <= cols, s, NEG_BIG)
        return s

    def keep(hh, s):
        s_ref[hh] = s
        bmax_ref[hh:hh + 1, :] = jnp.max(s, axis=0, keepdims=True)

    def block(j, j_next):
        for hh in range(g):
            s_next = scores(hh, j_next, False)
            m = m_ref[hh:hh + 1, :]
            m_new = jnp.maximum(m, bmax_ref[hh:hh + 1, :])
            alpha = jnp.exp2(m - m_new)
            p = jnp.exp2(s_ref[hh] - m_new).astype(BF16)
            m_ref[hh:hh + 1, :] = m_new
            vt = jnp.concatenate([vt_ref[j, hh * HEAD_DIM:(hh + 1) * HEAD_DIM, :], den_rows], axis=0)
            acc_ref[hh] = alpha * acc_ref[hh] + jnp.dot(vt, p, preferred_element_type=F32)
            keep(hh, s_next)

    for hh in range(g):
        keep(hh, scores(hh, i, True))
    acc_ref[...] = jnp.zeros(acc_ref.shape, F32)
    m_ref[...] = jnp.full(m_ref.shape, NEG_BIG, F32)
    block(i, 0)
    last = jnp.maximum(i - 1, 0)

    def run_blocks(first, count):
        for u in range(count):
            block(first + u, jnp.minimum(first + u + 1, last))

    @pl.loop(0, i // 4)
    def _(n):
        run_blocks(4 * n, 4)

    @pl.when(i % 4 >= 2)
    def _():
        run_blocks(i - i % 4, 2)

    @pl.when(i % 2 == 1)
    def _():
        run_blocks(i - 1, 1)

    for hh in range(g):
        o_ref[hh * HEAD_DIM:(hh + 1) * HEAD_DIM, lanes] = (
            acc_ref[hh, :HEAD_DIM, :] / acc_ref[hh, HEAD_DIM:HEAD_DIM + 1, :]).astype(o_ref.dtype)


def _attn_call(qt, k, vt, tq, aug=None):
    b, d, s = qt.shape
    nk = s // tq
    step = tq * ATTN_SUBTILES
    q_spec = pl.BlockSpec((None, d, step), lambda bi, i: (bi, 0, i))
    k_spec = pl.BlockSpec((None, s, d), lambda bi, i: (bi, 0, 0))
    vt_spec = pl.BlockSpec((None, nk, d, tq), lambda bi, i: (bi, 0, 0, 0))
    if aug is None:
        body, name, kdim, acc_rows, n_row_states = _sb_kernel, "sb_attn", PAIR, HEAD_DIM, 1
        extra = [pltpu.VMEM((N_HEADS, tq, tq), F32)]
        operands, in_specs = (qt, k, vt), [q_spec, k_spec, vt_spec]
    else:
        augk, augqt = aug
        body, name, kdim, acc_rows = _fox_kernel, "fox_attn", 2 * PAIR, HEAD_DIM + DEN_ROWS
        n_row_states, extra = 2, []
        operands = (qt, augqt, k, augk, vt)
        in_specs = [q_spec, pl.BlockSpec((None, N_HEADS * AUG, step), lambda bi, i: (bi, 0, i)),
                    k_spec, pl.BlockSpec((None, s, N_HEADS * AUG), lambda bi, i: (bi, 0, 0)),
                    vt_spec]
    return pl.pallas_call(
        functools.partial(body, tq=tq, subtiles=ATTN_SUBTILES),
        grid=(b, s // step),
        in_specs=in_specs,
        out_specs=q_spec,
        out_shape=jax.ShapeDtypeStruct((b, d, s), BF16),
        scratch_shapes=[pltpu.VMEM((N_HEADS, kdim, tq), BF16),
                        pltpu.VMEM((N_HEADS, tq, tq), F32),
                        pltpu.VMEM((N_HEADS, acc_rows, tq), F32)]
        + [pltpu.VMEM((N_HEADS, tq), F32)] * n_row_states + extra,
        compiler_params=pltpu.CompilerParams(
            dimension_semantics=("parallel", "arbitrary"), vmem_limit_bytes=V7X_VMEM_LIMIT),
        name=name,
    )(*operands)


def _post_kernel(x_ref, yst_ref, yft_ref, gate_ref, mod_ref, gffn_ref, gfin_ref, ws_ref, wfx_ref,
                 wo_ref, wg_ref, wu_ref, wd_ref, o_ref, *, bounds):
    d = x_ref.shape[-1]
    mod = mod_ref[...]
    a = jnp.dot(yst_ref[...].T, ws_ref[...], preferred_element_type=F32)
    b = jnp.dot(yft_ref[...].T, wfx_ref[...], preferred_element_type=F32)
    gate = gate_ref[...].astype(F32)
    merged = gate[:, :d] * a + gate[:, d:] * b
    x1 = x_ref[...] + mod[2:3, :] * jnp.dot(merged.astype(BF16), wo_ref[...],
                                            preferred_element_type=F32)
    h2 = _rms_modulate(x1, gffn_ref[...], mod[4:5, :], mod[3:4, :]).astype(BF16)
    ffn = None
    for lo, hi in zip(bounds[:-1], bounds[1:]):
        sl = slice(lo, hi)
        gt = jnp.dot(h2, wg_ref[:, sl], preferred_element_type=F32)
        up = jnp.dot(h2, wu_ref[:, sl], preferred_element_type=F32)
        act = (gt * jax.nn.sigmoid(gt) * up).astype(BF16)
        part = jnp.dot(act, wd_ref[sl, :], preferred_element_type=F32)
        ffn = part if ffn is None else ffn + part
    x2 = x1 + mod[5:6, :] * ffn
    ms = jnp.mean(x2 * x2, axis=-1, keepdims=True)
    o_ref[...] = x2 * lax.rsqrt(ms + RMS_EPS) * gfin_ref[...]


def _post_call(x, yt_sb, yt_fx, gates, mod3, g_ffn, g_final, w_bs, w_bf, w_out, w_gate, w_up,
               w_down, tm):
    b, s, d = x.shape
    row = lambda bi, i: (bi, i, 0)
    yt_spec = pl.BlockSpec((None, D_HEADS, tm), lambda bi, i: (bi, 0, i))
    dff = w_gate.shape[1]
    half = -(-dff // (2 * V7X_MXU_WIDTH)) * V7X_MXU_WIDTH
    bounds = (0, half, dff) if half < dff else (0, dff)
    weights = (w_bs, w_bf, w_out, w_gate, w_up, w_down)
    return pl.pallas_call(
        functools.partial(_post_kernel, bounds=bounds),
        grid=(b, s // tm),
        in_specs=[pl.BlockSpec((None, tm, d), row), yt_spec, yt_spec,
                  pl.BlockSpec((None, tm, 2 * d), row),
                  pl.BlockSpec((None, N_MOD, d), lambda bi, i: (bi, 0, 0)),
                  _const_spec((1, d)), _const_spec((1, d))]
        + [_const_spec(w.shape) for w in weights],
        out_specs=pl.BlockSpec((None, tm, d), row),
        out_shape=jax.ShapeDtypeStruct((b, s, d), F32),
        compiler_params=pltpu.CompilerParams(
            dimension_semantics=("parallel", "parallel"), vmem_limit_bytes=V7X_VMEM_LIMIT),
        name="post",
    )(x, yt_sb, yt_fx, gates, mod3, g_ffn, g_final, *weights)


def kernel(x, c, w_ada, b_ada, g_mix, w_in, b_forget, b_gate, w_branch_sb, w_branch_fox,
           w_out, g_ffn, w_ffn_gate, w_ffn_up, w_ffn_down, g_final):
    b, s, d = x.shape
    depth = w_ada.shape[0]
    assert depth == 1, "the final RMSNorm is fused into the last (only) layer's closing call"
    tm = min(512, s)
    tq = min(256, s)
    dh = D_HEADS
    for l in range(depth):
        mod3 = _mod_call(c, w_ada, b_ada, l).reshape(b, N_MOD, d)
        w = w_in[l]
        w_k = jnp.concatenate([w[:, dh:2 * dh], w[:, 4 * dh:5 * dh]], axis=1).astype(BF16)
        w_qvt = jnp.concatenate([w[:, 0:dh], w[:, 3 * dh:4 * dh], w[:, 2 * dh:3 * dh],
                                 w[:, 5 * dh:6 * dh], w[:, 6 * dh:6 * dh + N_HEADS],
                                 jnp.zeros((d, F_PAD_ROWS - N_HEADS), w.dtype)],
                                axis=1).T.astype(BF16)
        w_g = w[:, 6 * dh + N_HEADS:].astype(BF16)
        k_sb, k_fx, qt_sb, qt_fx, vt_sb, vt_fx, lf_t, gates = _inproj_call(
            x, mod3, g_mix[l].reshape(1, d), w_k, w_qvt, w_g, b_forget[l].reshape(N_HEADS, 1),
            b_gate[l].reshape(1, 2 * d), tm, tq)

        yt_sb = _attn_call(qt_sb, k_sb, vt_sb, tq)
        aug = _cumsum_call(lf_t)
        yt_fx = _attn_call(qt_fx, k_fx, vt_fx, tq, aug)

        x = _post_call(x, yt_sb, yt_fx, gates, mod3, g_ffn[l].reshape(1, d), g_final.reshape(1, d),
                       w_branch_sb[l].astype(BF16), w_branch_fox[l].astype(BF16),
                       w_out[l].astype(BF16), w_ffn_gate[l].astype(BF16),
                       w_ffn_up[l].astype(BF16), w_ffn_down[l].astype(BF16), tm)
    return x
```

```python
import functools
import math

import jax
import jax.numpy as jnp
from jax import lax
from jax.experimental import pallas as pl
from jax.experimental.pallas import tpu as pltpu

F32 = jnp.float32
BF16 = jnp.bfloat16

HEAD_DIM = 64
N_HEADS = 8
D_HEADS = N_HEADS * HEAD_DIM
PAIR = 2 * HEAD_DIM
N_MOD = 6
RMS_EPS = 1e-6
LOG2E = math.log2(math.e)
SB_Q_SCALE = HEAD_DIM ** -0.5
FOX_Q_SCALE = LOG2E * HEAD_DIM ** -0.5
AUG = 16
DEN_ROWS = 16
WEIGHT_LAG = 3
F_PAD_ROWS = 16
NEG_BIG = -1e30
SB_DEAD = 160.0 * math.log(2.0)
V7X_VMEM_LIMIT = 56 * 1024 * 1024
V7X_MXU_WIDTH = 256
NT_DIMS = (((1,), (1,)), ((), ()))


def _rms_modulate(x, g, scale, shift):
    ms = jnp.mean(x * x, axis=-1, keepdims=True)
    y = x * lax.rsqrt(ms + RMS_EPS) * g
    return y * (1.0 + scale) + shift


def _const_spec(shape):
    return pl.BlockSpec(shape, lambda *_: (0,) * len(shape), pipeline_mode=pl.Buffered(1))


def _mod_kernel(c_ref, w_ref, b_ref, o_ref):
    c = c_ref[...]
    ca = c * jax.nn.sigmoid(c)
    o_ref[...] = jnp.dot(ca, w_ref[...], preferred_element_type=F32,
                         precision=lax.Precision.HIGHEST) + b_ref[...]


def _mod_call(c, w_ada, b_ada, layer):
    b, d = c.shape
    depth, _, n = w_ada.shape
    tn = 1024
    return pl.pallas_call(
        _mod_kernel,
        grid=(n // tn,),
        in_specs=[pl.BlockSpec((b, d), lambda j: (0, 0)),
                  pl.BlockSpec((None, d, tn), lambda j: (layer, 0, j)),
                  pl.BlockSpec((None, 1, tn), lambda j: (layer, 0, j))],
        out_specs=pl.BlockSpec((b, tn), lambda j: (0, j)),
        out_shape=jax.ShapeDtypeStruct((b, n), F32),
        name="mod",
    )(c, w_ada, b_ada.reshape(depth, 1, n))


def _inproj_kernel(x_ref, mod_ref, g_ref, wk_ref, wqvt_ref, wg_ref, bf_ref, bg_ref,
                   ks_ref, kf_ref, qts_ref, qtf_ref, vts_ref, vtf_ref, lft_ref, gate_ref, *, tk):
    mod = mod_ref[...]
    h = _rms_modulate(x_ref[...], g_ref[...], mod[1:2, :], mod[0:1, :]).astype(BF16)
    d = D_HEADS
    kk = jnp.dot(h, wk_ref[...], preferred_element_type=F32)
    ks_ref[...] = kk[:, :d].astype(BF16)
    kf_ref[...] = kk[:, d:].astype(BF16)
    t = lax.dot_general(wqvt_ref[...], h, NT_DIMS, preferred_element_type=F32)
    qts_ref[...] = (t[0 * d:1 * d] * SB_Q_SCALE).astype(BF16)
    qtf_ref[...] = (t[1 * d:2 * d] * FOX_Q_SCALE).astype(BF16)
    for c in range(vts_ref.shape[0]):
        vts_ref[c] = t[2 * d:3 * d, c * tk:(c + 1) * tk].astype(BF16)
        vtf_ref[c] = t[3 * d:4 * d, c * tk:(c + 1) * tk].astype(BF16)
    f = t[4 * d:4 * d + N_HEADS] + bf_ref[...]
    lft_ref[...] = -(jnp.maximum(-f, 0.0) + jnp.log(1.0 + jnp.exp(-jnp.abs(f)))) * LOG2E
    gl = jnp.dot(h, wg_ref[...], preferred_element_type=F32) + bg_ref[...]
    gate_ref[...] = jax.nn.sigmoid(gl).astype(BF16)


def _inproj_call(x, mod3, g_mix, w_k, w_qvt, w_g, b_f, b_g, tm, tk):
    b, s, d = x.shape
    row = lambda bi, i: (bi, i, 0)
    col = lambda bi, i: (bi, 0, i)
    k_spec = pl.BlockSpec((None, tm, D_HEADS), row)
    qt_spec = pl.BlockSpec((None, D_HEADS, tm), col)
    vt_spec = pl.BlockSpec((None, tm // tk, D_HEADS, tk), lambda bi, i: (bi, i, 0, 0))
    return pl.pallas_call(
        functools.partial(_inproj_kernel, tk=tk),
        grid=(b, s // tm),
        in_specs=[pl.BlockSpec((None, tm, d), row),
                  pl.BlockSpec((None, N_MOD, d), lambda bi, i: (bi, 0, 0)),
                  _const_spec((1, d)),
                  _const_spec(w_k.shape), _const_spec(w_qvt.shape),
                  _const_spec(w_g.shape), _const_spec(b_f.shape), _const_spec(b_g.shape)],
        out_specs=[k_spec, k_spec, qt_spec, qt_spec, vt_spec, vt_spec,
                   pl.BlockSpec((None, N_HEADS, tm), col),
                   pl.BlockSpec((None, tm, 2 * d), row)],
        out_shape=[jax.ShapeDtypeStruct((b, s, D_HEADS), BF16)] * 2
        + [jax.ShapeDtypeStruct((b, D_HEADS, s), BF16)] * 2
        + [jax.ShapeDtypeStruct((b, s // tk, D_HEADS, tk), BF16)] * 2
        + [jax.ShapeDtypeStruct((b, N_HEADS, s), F32),
           jax.ShapeDtypeStruct((b, s, 2 * d), BF16)],
        compiler_params=pltpu.CompilerParams(
            dimension_semantics=("parallel", "parallel"), vmem_limit_bytes=V7X_VMEM_LIMIT),
        name="inproj",
    )(x, mod3, g_mix, w_k, w_qvt, w_g, b_f, b_g)


def _split3(v):
    hi = v.astype(BF16).astype(F32)
    r = v - hi
    mid = r.astype(BF16).astype(F32)
    lo = (r - mid).astype(BF16).astype(F32)
    return hi, mid, lo


def _cumsum_kernel(lft_ref, augk_ref, augqt_ref, *, chunk):
    h, s = lft_ref.shape
    n3 = 3 * h
    r = lax.broadcasted_iota(jnp.int32, (chunk, chunk), 0)
    c = lax.broadcasted_iota(jnp.int32, (chunk, chunk), 1)
    upper = (r <= c).astype(BF16)

    qr = lax.broadcasted_iota(jnp.int32, (h * AUG, n3), 0)
    qc = lax.broadcasted_iota(jnp.int32, (h * AUG, n3), 1)
    qrow = lax.broadcasted_iota(jnp.int32, (h * AUG, 1), 0)
    place_q = jnp.zeros((h * AUG, n3), F32)
    place_k = jnp.zeros((h * AUG, n3), F32)
    ones_q = jnp.zeros((h * AUG, 1), F32)
    ones_k = jnp.zeros((h * AUG, 1), F32)
    for hh in range(h):
        for p in range(3):
            place_q = jnp.where((qr == hh * AUG + 3 + p) & (qc == p * h + hh), 1.0, place_q)
            place_k = jnp.where((qr == hh * AUG + p) & (qc == p * h + hh), -1.0, place_k)
            ones_q = jnp.where(qrow == hh * AUG + p, 1.0, ones_q)
            ones_k = jnp.where(qrow == hh * AUG + 3 + p, 1.0, ones_k)
    place_q = place_q.astype(BF16)
    place_k = place_k.astype(BF16)

    carry = jnp.zeros((h, 1), F32)
    for ci in range(s // chunk):
        sl = slice(ci * chunk, (ci + 1) * chunk)
        pieces = jnp.concatenate(_split3(lft_ref[:, sl]), axis=0).astype(BF16)
        p3 = jnp.dot(pieces, upper, preferred_element_type=F32)
        cum = (p3[0:h] + p3[h:2 * h]) + p3[2 * h:n3] + carry
        carry = cum[:, chunk - 1:chunk]
        stacked = jnp.concatenate(_split3(cum), axis=0).astype(BF16)
        augqt_ref[:, sl] = (jnp.dot(place_q, stacked, preferred_element_type=F32)
                            + ones_q).astype(BF16)
        augkt = jnp.dot(place_k, stacked, preferred_element_type=F32) + ones_k
        augk_ref[sl, :] = augkt.T.astype(BF16)


def _cumsum_call(lf_t):
    b, h, s = lf_t.shape
    return pl.pallas_call(
        functools.partial(_cumsum_kernel, chunk=min(256, s)),
        grid=(b,),
        in_specs=[pl.BlockSpec((None, h, s), lambda bi: (bi, 0, 0))],
        out_specs=[pl.BlockSpec((None, s, h * AUG), lambda bi: (bi, 0, 0)),
                   pl.BlockSpec((None, h * AUG, s), lambda bi: (bi, 0, 0))],
        out_shape=[jax.ShapeDtypeStruct((b, s, h * AUG), BF16),
                   jax.ShapeDtypeStruct((b, h * AUG, s), BF16)],
        compiler_params=pltpu.CompilerParams(dimension_semantics=("parallel",)),
        name="cumsum",
    )(lf_t)


def _softplus(z):
    return jnp.maximum(z, 0.0) + jnp.log(1.0 + jnp.exp(-jnp.abs(z)))


def _fill_query_operand(rhs_ref, qt_ref, augqt_ref, first_tile):
    @pl.when(first_tile)
    def _():
        rhs_ref[...] = jnp.zeros(rhs_ref.shape, rhs_ref.dtype)

    for hh in range(N_HEADS):
        half = hh % 2
        rhs_ref[hh, half * HEAD_DIM:(half + 1) * HEAD_DIM, :] = (
            qt_ref[hh * HEAD_DIM:(hh + 1) * HEAD_DIM, :])
        if augqt_ref is not None:
            rhs_ref[hh, PAIR + hh * AUG:PAIR + (hh + 1) * AUG, :] = (
                augqt_ref[hh * AUG:(hh + 1) * AUG, :])


def _sb_kernel(qt_ref, k_ref, vt_ref, o_ref, rhs_ref, z_ref, acc_ref, run_ref, c_ref, *, tq):
    g = N_HEADS
    i = pl.program_id(1)
    _fill_query_operand(rhs_ref, qt_ref, None, i == 0)
    rows = lax.broadcasted_iota(jnp.int32, (tq, tq), 0)
    cols = lax.broadcasted_iota(jnp.int32, (tq, tq), 1)
    suffix = (cols >= rows).astype(BF16)

    def scores(hh, j, diagonal):
        start = pl.multiple_of(j * tq, tq)
        z = jnp.dot(k_ref[pl.ds(start, tq), (hh // 2) * PAIR:(hh // 2 + 1) * PAIR], rhs_ref[hh],
                    preferred_element_type=F32)
        return jnp.where(rows < cols, z, NEG_BIG) if diagonal else z

    def block(j, j_next):
        runs = [run_ref[hh:hh + 1, :] for hh in range(g)]
        for t in range(g + WEIGHT_LAG):
            if t < g:
                sp = _softplus(z_ref[t].astype(BF16))
                c_ref[t] = jnp.dot(suffix, sp, preferred_element_type=F32)
            if t >= WEIGHT_LAG:
                hh = t - WEIGHT_LAG
                z_next = scores(hh, j_next, False)
                w = jnp.exp(z_ref[hh] - c_ref[hh] - runs[hh])
                acc_ref[hh] += jnp.dot(vt_ref[j, hh * HEAD_DIM:(hh + 1) * HEAD_DIM, :],
                                       w.astype(BF16), preferred_element_type=F32)
                run_ref[hh:hh + 1, :] = runs[hh] + c_ref[hh, 0:1, :]
                z_ref[hh] = z_next
        return jnp.min(run_ref[...])

    for hh in range(g):
        z_ref[hh] = scores(hh, i, True)
    acc_ref[...] = jnp.zeros(acc_ref.shape, F32)
    run_ref[...] = jnp.zeros(run_ref.shape, F32)
    lowest = block(i, jnp.maximum(i - 1, 0))

    def cond(state):
        n, lowest = state
        return jnp.logical_and(n < i, lowest < SB_DEAD)

    def body(state):
        n, _ = state
        j = i - 1 - n
        return n + 1, block(j, jnp.maximum(j - 1, 0))

    lax.while_loop(cond, body, (jnp.int32(0), lowest))
    for hh in range(g):
        o_ref[hh * HEAD_DIM:(hh + 1) * HEAD_DIM, :] = acc_ref[hh].astype(o_ref.dtype)


def _fox_kernel(qt_ref, augqt_ref, k_ref, augk_ref, vt_ref, o_ref, rhs_ref, s_ref, acc_ref, m_ref,
                bmax_ref, *, tq):
    g = N_HEADS
    i = pl.program_id(1)
    _fill_query_operand(rhs_ref, qt_ref, augqt_ref, i == 0)
    den_rows = jnp.ones((DEN_ROWS, tq), BF16)

    def scores(hh, j, diagonal):
        start = pl.multiple_of(j * tq, tq)
        pair = slice((hh // 2) * PAIR, (hh // 2 + 1) * PAIR)
        lhs = jnp.concatenate([k_ref[pl.ds(start, tq), pair], augk_ref[pl.ds(start, tq), :]],
                              axis=1)
        s = jnp.dot(lhs, rhs_ref[hh], preferred_element_type=F32)
        if diagonal:
            rows = lax.broadcasted_iota(jnp.int32, (tq, tq), 0)
            cols = lax.broadcasted_iota(jnp.int32, (tq, tq), 1)
            s = jnp.where(rows <= cols, s, NEG_BIG)
        return s

    def keep(hh, s):
        s_ref[hh] = s
        bmax_ref[hh:hh + 1, :] = jnp.max(s, axis=0, keepdims=True)

    def block(j, j_next):
        for hh in range(g):
            s_next = scores(hh, j_next, False)
            m = m_ref[hh:hh + 1, :]
            m_new = jnp.maximum(m, bmax_ref[hh:hh + 1, :])
            alpha = jnp.exp2(m - m_new)
            p = jnp.exp2(s_ref[hh] - m_new).astype(BF16)
            m_ref[hh:hh + 1, :] = m_new
            vt = jnp.concatenate([vt_ref[j, hh * HEAD_DIM:(hh + 1) * HEAD_DIM, :], den_rows], axis=0)
            acc_ref[hh] = alpha * acc_ref[hh] + jnp.dot(vt, p, preferred_element_type=F32)
            keep(hh, s_next)

    for hh in range(g):
        keep(hh, scores(hh, i, True))
    acc_ref[...] = jnp.zeros(acc_ref.shape, F32)
    m_ref[...] = jnp.full(m_ref.shape, NEG_BIG, F32)
    block(i, 0)
    last = jnp.maximum(i - 1, 0)

    def run_blocks(first, count):
        for u in range(count):
            block(first + u, jnp.minimum(first + u + 1, last))

    @pl.loop(0, i // 4)
    def _(n):
        run_blocks(4 * n, 4)

    @pl.when(i % 4 >= 2)
    def _():
        run_blocks(i - i % 4, 2)

    @pl.when(i % 2 == 1)
    def _():
        run_blocks(i - 1, 1)

    for hh in range(g):
        o_ref[hh * HEAD_DIM:(hh + 1) * HEAD_DIM, :] = (
            acc_ref[hh, :HEAD_DIM, :] / acc_ref[hh, HEAD_DIM:HEAD_DIM + 1, :]).astype(o_ref.dtype)


def _attn_call(qt, k, vt, tq, aug=None):
    b, d, s = qt.shape
    nk = s // tq
    q_spec = pl.BlockSpec((None, d, tq), lambda bi, i: (bi, 0, i))
    k_spec = pl.BlockSpec((None, s, d), lambda bi, i: (bi, 0, 0))
    vt_spec = pl.BlockSpec((None, nk, d, tq), lambda bi, i: (bi, 0, 0, 0))
    if aug is None:
        body, name, kdim, acc_rows, n_row_states = _sb_kernel, "sb_attn", PAIR, HEAD_DIM, 1
        extra = [pltpu.VMEM((N_HEADS, tq, tq), F32)]
        operands, in_specs = (qt, k, vt), [q_spec, k_spec, vt_spec]
    else:
        augk, augqt = aug
        body, name, kdim, acc_rows = _fox_kernel, "fox_attn", 2 * PAIR, HEAD_DIM + DEN_ROWS
        n_row_states, extra = 2, []
        operands = (qt, augqt, k, augk, vt)
        in_specs = [q_spec, pl.BlockSpec((None, N_HEADS * AUG, tq), lambda bi, i: (bi, 0, i)),
                    k_spec, pl.BlockSpec((None, s, N_HEADS * AUG), lambda bi, i: (bi, 0, 0)),
                    vt_spec]
    return pl.pallas_call(
        functools.partial(body, tq=tq),
        grid=(b, s // tq),
        in_specs=in_specs,
        out_specs=q_spec,
        out_shape=jax.ShapeDtypeStruct((b, d, s), BF16),
        scratch_shapes=[pltpu.VMEM((N_HEADS, kdim, tq), BF16),
                        pltpu.VMEM((N_HEADS, tq, tq), F32),
                        pltpu.VMEM((N_HEADS, acc_rows, tq), F32)]
        + [pltpu.VMEM((N_HEADS, tq), F32)] * n_row_states + extra,
        compiler_params=pltpu.CompilerParams(
            dimension_semantics=("parallel", "arbitrary"), vmem_limit_bytes=V7X_VMEM_LIMIT),
        name=name,
    )(*operands)


def _attn_fused_kernel(qts_ref, ks_ref, vts_ref, qtf_ref, augqt_ref, kf_ref, augk_ref, vtf_ref,
                       os_ref, of_ref, rhss_ref, rhsf_ref, z_ref, c_ref, s_ref, accs_ref, accf_ref,
                       run_ref, m_ref, bmax_ref, *, tq):
    g = N_HEADS
    i = pl.program_id(1)
    _fill_query_operand(rhss_ref, qts_ref, None, i == 0)
    _fill_query_operand(rhsf_ref, qtf_ref, augqt_ref, i == 0)
    rows = lax.broadcasted_iota(jnp.int32, (tq, tq), 0)
    cols = lax.broadcasted_iota(jnp.int32, (tq, tq), 1)
    suffix = (cols >= rows).astype(BF16)
    den_rows = jnp.ones((DEN_ROWS, tq), BF16)

    def pair_lanes(hh):
        return slice((hh // 2) * PAIR, (hh // 2 + 1) * PAIR)

    def head_rows(hh):
        return slice(hh * HEAD_DIM, (hh + 1) * HEAD_DIM)

    def sb_scores(hh, j, diagonal):
        start = pl.multiple_of(j * tq, tq)
        z = jnp.dot(ks_ref[pl.ds(start, tq), pair_lanes(hh)], rhss_ref[hh],
                    preferred_element_type=F32)
        return jnp.where(rows < cols, z, NEG_BIG) if diagonal else z

    def fox_scores(hh, j, diagonal):
        start = pl.multiple_of(j * tq, tq)
        lhs = jnp.concatenate([kf_ref[pl.ds(start, tq), pair_lanes(hh)],
                               augk_ref[pl.ds(start, tq), :]], axis=1)
        s = jnp.dot(lhs, rhsf_ref[hh], preferred_element_type=F32)
        return jnp.where(rows <= cols, s, NEG_BIG) if diagonal else s

    def keep(hh, s):
        s_ref[hh] = s
        bmax_ref[hh:hh + 1, :] = jnp.max(s, axis=0, keepdims=True)

    def fox_head(hh, j, j_next):
        s_next = fox_scores(hh, j_next, False)
        m = m_ref[hh:hh + 1, :]
        m_new = jnp.maximum(m, bmax_ref[hh:hh + 1, :])
        alpha = jnp.exp2(m - m_new)
        p = jnp.exp2(s_ref[hh] - m_new).astype(BF16)
        m_ref[hh:hh + 1, :] = m_new
        vt = jnp.concatenate([vtf_ref[j, head_rows(hh), :], den_rows], axis=0)
        accf_ref[hh] = alpha * accf_ref[hh] + jnp.dot(vt, p, preferred_element_type=F32)
        keep(hh, s_next)

    def fused_trip(j, j_next):
        runs = [run_ref[hh:hh + 1, :] for hh in range(g)]
        for t in range(g + WEIGHT_LAG):
            if t < g:
                sp = _softplus(z_ref[t].astype(BF16))
                c_ref[t] = jnp.dot(suffix, sp, preferred_element_type=F32)
                fox_head(t, j, j_next)
            if t >= WEIGHT_LAG:
                hh = t - WEIGHT_LAG
                z_next = sb_scores(hh, j_next, False)
                w = jnp.exp(z_ref[hh] - c_ref[hh] - runs[hh])
                accs_ref[hh] += jnp.dot(vts_ref[j, head_rows(hh), :], w.astype(BF16),
                                        preferred_element_type=F32)
                run_ref[hh:hh + 1, :] = runs[hh] + c_ref[hh, 0:1, :]
                z_ref[hh] = z_next
        return jnp.min(run_ref[...])

    for hh in range(g):
        z_ref[hh] = sb_scores(hh, i, True)
        keep(hh, fox_scores(hh, i, True))
    accs_ref[...] = jnp.zeros(accs_ref.shape, F32)
    accf_ref[...] = jnp.zeros(accf_ref.shape, F32)
    run_ref[...] = jnp.zeros(run_ref.shape, F32)
    m_ref[...] = jnp.full(m_ref.shape, NEG_BIG, F32)
    lowest = fused_trip(i, jnp.maximum(i - 1, 0))

    def cond(state):
        n, lowest = state
        return jnp.logical_and(n < i, lowest < SB_DEAD)

    def body(state):
        n, _ = state
        j = i - 1 - n
        return n + 1, fused_trip(j, jnp.maximum(j - 1, 0))

    n_done, _ = lax.while_loop(cond, body, (jnp.int32(0), lowest))
    left = i - n_done

    def run_blocks(first, count):
        for u in range(count):
            j = first - u
            for hh in range(g):
                fox_head(hh, j, jnp.maximum(j - 1, 0))

    @pl.loop(0, left // 4)
    def _(n):
        run_blocks(left - 1 - 4 * n, 4)

    @pl.when(left % 4 >= 2)
    def _():
        run_blocks(left % 4 - 1, 2)

    @pl.when(left % 2 == 1)
    def _():
        run_blocks(0, 1)

    for hh in range(g):
        os_ref[head_rows(hh), :] = accs_ref[hh].astype(os_ref.dtype)
        of_ref[head_rows(hh), :] = (accf_ref[hh, :HEAD_DIM, :]
                                    / accf_ref[hh, HEAD_DIM:HEAD_DIM + 1, :]).astype(of_ref.dtype)


def _attn_fused_call(qt_sb, k_sb, vt_sb, qt_fx, augqt, k_fx, augk, vt_fx, tq):
    b, d, s = qt_sb.shape
    nk = s // tq
    q_spec = pl.BlockSpec((None, d, tq), lambda bi, i: (bi, 0, i))
    aq_spec = pl.BlockSpec((None, N_HEADS * AUG, tq), lambda bi, i: (bi, 0, i))
    k_spec = pl.BlockSpec((None, s, d), lambda bi, i: (bi, 0, 0))
    ak_spec = pl.BlockSpec((None, s, N_HEADS * AUG), lambda bi, i: (bi, 0, 0))
    vt_spec = pl.BlockSpec((None, nk, d, tq), lambda bi, i: (bi, 0, 0, 0),
                           pipeline_mode=pl.Buffered(1))
    tile = pltpu.VMEM((N_HEADS, tq, tq), F32)
    row = pltpu.VMEM((N_HEADS, tq), F32)
    return pl.pallas_call(
        functools.partial(_attn_fused_kernel, tq=tq),
        grid=(b, s // tq),
        in_specs=[q_spec, k_spec, vt_spec, q_spec, aq_spec, k_spec, ak_spec, vt_spec],
        out_specs=[q_spec, q_spec],
        out_shape=[jax.ShapeDtypeStruct((b, d, s), BF16)] * 2,
        scratch_shapes=[pltpu.VMEM((N_HEADS, PAIR, tq), BF16), pltpu.VMEM((N_HEADS, 2 * PAIR, tq), BF16),
                        tile, tile, tile,
                        pltpu.VMEM((N_HEADS, HEAD_DIM, tq), F32),
                        pltpu.VMEM((N_HEADS, HEAD_DIM + DEN_ROWS, tq), F32), row, row, row],
        compiler_params=pltpu.CompilerParams(
            dimension_semantics=("parallel", "arbitrary"), vmem_limit_bytes=V7X_VMEM_LIMIT),
        name="attn",
    )(qt_sb, k_sb, vt_sb, qt_fx, augqt, k_fx, augk, vt_fx)


def _post_kernel(x_ref, yst_ref, yft_ref, gate_ref, mod_ref, gffn_ref, gfin_ref, ws_ref, wfx_ref,
                 wo_ref, wg_ref, wu_ref, wd_ref, o_ref, *, bounds):
    d = x_ref.shape[-1]
    mod = mod_ref[...]
    a = jnp.dot(yst_ref[...].T, ws_ref[...], preferred_element_type=F32)
    b = jnp.dot(yft_ref[...].T, wfx_ref[...], preferred_element_type=F32)
    gate = gate_ref[...].astype(F32)
    merged = gate[:, :d] * a + gate[:, d:] * b
    x1 = x_ref[...] + mod[2:3, :] * jnp.dot(merged.astype(BF16), wo_ref[...],
                                            preferred_element_type=F32)
    h2 = _rms_modulate(x1, gffn_ref[...], mod[4:5, :], mod[3:4, :]).astype(BF16)
    ffn = None
    for lo, hi in zip(bounds[:-1], bounds[1:]):
        sl = slice(lo, hi)
        gt = jnp.dot(h2, wg_ref[:, sl], preferred_element_type=F32)
        up = jnp.dot(h2, wu_ref[:, sl], preferred_element_type=F32)
        act = (gt * jax.nn.sigmoid(gt) * up).astype(BF16)
        part = jnp.dot(act, wd_ref[sl, :], preferred_element_type=F32)
        ffn = part if ffn is None else ffn + part
    x2 = x1 + mod[5:6, :] * ffn
    ms = jnp.mean(x2 * x2, axis=-1, keepdims=True)
    o_ref[...] = x2 * lax.rsqrt(ms + RMS_EPS) * gfin_ref[...]


def _post_call(x, yt_sb, yt_fx, gates, mod3, g_ffn, g_final, w_bs, w_bf, w_out, w_gate, w_up,
               w_down, tm):
    b, s, d = x.shape
    row = lambda bi, i: (bi, i, 0)
    yt_spec = pl.BlockSpec((None, D_HEADS, tm), lambda bi, i: (bi, 0, i))
    dff = w_gate.shape[1]
    half = -(-dff // (2 * V7X_MXU_WIDTH)) * V7X_MXU_WIDTH
    bounds = (0, half, dff) if half < dff else (0, dff)
    weights = (w_bs, w_bf, w_out, w_gate, w_up, w_down)
    return pl.pallas_call(
        functools.partial(_post_kernel, bounds=bounds),
        grid=(b, s // tm),
        in_specs=[pl.BlockSpec((None, tm, d), row), yt_spec, yt_spec,
                  pl.BlockSpec((None, tm, 2 * d), row),
                  pl.BlockSpec((None, N_MOD, d), lambda bi, i: (bi, 0, 0)),
                  _const_spec((1, d)), _const_spec((1, d))]
        + [_const_spec(w.shape) for w in weights],
        out_specs=pl.BlockSpec((None, tm, d), row),
        out_shape=jax.ShapeDtypeStruct((b, s, d), F32),
        compiler_params=pltpu.CompilerParams(
            dimension_semantics=("parallel", "parallel"), vmem_limit_bytes=V7X_VMEM_LIMIT),
        name="post",
    )(x, yt_sb, yt_fx, gates, mod3, g_ffn, g_final, *weights)


def kernel(x, c, w_ada, b_ada, g_mix, w_in, b_forget, b_gate, w_branch_sb, w_branch_fox,
           w_out, g_ffn, w_ffn_gate, w_ffn_up, w_ffn_down, g_final):
    b, s, d = x.shape
    depth = w_ada.shape[0]
    assert depth == 1, "the final RMSNorm is fused into the last (only) layer's closing call"
    tm = min(512, s)
    tq = min(256, s)
    dh = D_HEADS
    for l in range(depth):
        mod3 = _mod_call(c, w_ada, b_ada, l).reshape(b, N_MOD, d)
        w = w_in[l]
        w_k = jnp.concatenate([w[:, dh:2 * dh], w[:, 4 * dh:5 * dh]], axis=1).astype(BF16)
        w_qvt = jnp.concatenate([w[:, 0:dh], w[:, 3 * dh:4 * dh], w[:, 2 * dh:3 * dh],
                                 w[:, 5 * dh:6 * dh], w[:, 6 * dh:6 * dh + N_HEADS],
                                 jnp.zeros((d, F_PAD_ROWS - N_HEADS), w.dtype)],
                                axis=1).T.astype(BF16)
        w_g = w[:, 6 * dh + N_HEADS:].astype(BF16)
        k_sb, k_fx, qt_sb, qt_fx, vt_sb, vt_fx, lf_t, gates = _inproj_call(
            x, mod3, g_mix[l].reshape(1, d), w_k, w_qvt, w_g, b_forget[l].reshape(N_HEADS, 1),
            b_gate[l].reshape(1, 2 * d), tm, tq)

        augk, augqt = _cumsum_call(lf_t)
        yt_sb, yt_fx = _attn_fused_call(qt_sb, k_sb, vt_sb, qt_fx, augqt, k_fx, augk, vt_fx, tq)

        x = _post_call(x, yt_sb, yt_fx, gates, mod3, g_ffn[l].reshape(1, d), g_final.reshape(1, d),
                       w_branch_sb[l].astype(BF16), w_branch_fox[l].astype(BF16),
                       w_out[l].astype(BF16), w_ffn_gate[l].astype(BF16),
                       w_ffn_up[l].astype(BF16), w_ffn_down[l].astype(BF16), tm)
    return x
```
